```python
import math
import jax, jax.numpy as jnp
from jax import lax
import numpy as np

D_MODEL = 2048
BATCH = 4
SEQ = 4096
DEPTH = 4

HEAD_DIM = 128
SB_HEADS = D_MODEL // HEAD_DIM
DIL_CONFIGS = ((128, 1), (512, 4), (2048, 16))
N_DIL_GROUPS = len(DIL_CONFIGS)
DIL_HEADS_PER_GROUP = 6
DIL_HEADS = N_DIL_GROUPS * DIL_HEADS_PER_GROUP
D_FF = -(-(8 * D_MODEL) // (3 * 256)) * 256
BLOCK_Q = 128
ROPE_THETA = 10000.0
LN_EPS = 1e-5
DEEPNORM_ALPHA = (2.0 * DEPTH) ** 0.25
DEEPNORM_BETA = (8.0 * DEPTH) ** -0.25
N_SB_LAYERS = (DEPTH + 1) // 2
N_DIL_LAYERS = DEPTH // 2

kernel_name = 'stickbreak_dilated_deepnorm_hybrid'


def layer_norm(x, g, b):
    xf = x.astype(jnp.float32)
    mu = jnp.mean(xf, axis=-1, keepdims=True)
    var = jnp.mean(jnp.square(xf - mu), axis=-1, keepdims=True)
    y = (xf - mu) * lax.rsqrt(var + LN_EPS) * g.astype(jnp.float32) + b.astype(jnp.float32)
    return y.astype(x.dtype)


def rope_tables(seq_len, dim):
    pos = jnp.arange(seq_len, dtype=jnp.float32)
    inv_freq = ROPE_THETA ** (-jnp.arange(0, dim, 2, dtype=jnp.float32) / dim)
    ang = pos[:, None] * inv_freq[None, :]
    return jnp.cos(ang), jnp.sin(ang)


def apply_rope(x, cos, sin):
    xf = x.astype(jnp.float32)
    x1, x2 = jnp.split(xf, 2, axis=-1)
    c = cos[None, :, None, :]
    s = sin[None, :, None, :]
    return jnp.concatenate([x1 * c - x2 * s, x1 * s + x2 * c], axis=-1).astype(x.dtype)


def stick_breaking_attention(q, k, v):
    b, s, h, d = q.shape
    n_blocks = s // BLOCK_Q
    scale = 1.0 / math.sqrt(d)
    kpos = jnp.arange(s)

    def block(i):
        t0 = i * BLOCK_Q
        qb = lax.dynamic_slice_in_dim(q, t0, BLOCK_Q, axis=1)
        z = jnp.einsum('bqhd,bkhd->bhqk', qb, k, preferred_element_type=jnp.float32) * scale
        qpos = t0 + jnp.arange(BLOCK_Q)
        causal = kpos[None, :] < qpos[:, None]
        log_stay = jnp.where(causal, jax.nn.log_sigmoid(-z), 0.0)
        later = lax.cumsum(log_stay, axis=3, reverse=True) - log_stay
        w = jnp.where(causal, jnp.exp(jax.nn.log_sigmoid(z) + later), 0.0)
        o = jnp.einsum('bhqk,bkhd->bqhd', w, v.astype(jnp.float32))
        return o.astype(q.dtype)

    out = lax.map(block, jnp.arange(n_blocks))
    return jnp.moveaxis(out, 0, 1).reshape(b, s, h, d)


def dilated_window_attention(q, k, v, window, dilation):
    b, s, h, d = q.shape
    n_keys = window // dilation + 1
    n_blocks = s // BLOCK_Q
    scale = 1.0 / math.sqrt(d)
    steps = dilation * jnp.arange(n_keys)

    def block(i):
        t0 = i * BLOCK_Q
        qpos = t0 + jnp.arange(BLOCK_Q)
        idx = qpos[:, None] - steps[None, :]
        valid = idx >= 0
        flat = jnp.maximum(idx, 0).reshape(-1)
        kg = jnp.take(k, flat, axis=1).reshape(b, BLOCK_Q, n_keys, h, d)
        vg = jnp.take(v, flat, axis=1).reshape(b, BLOCK_Q, n_keys, h, d)
        qb = lax.dynamic_slice_in_dim(q, t0, BLOCK_Q, axis=1)
        sc = jnp.einsum('bqhd,bqmhd->bhqm', qb, kg, preferred_element_type=jnp.float32) * scale
        sc = jnp.where(valid[None, None], sc, -jnp.inf)
        m = jnp.max(sc, axis=-1, keepdims=True)
        p = jnp.exp(sc - m)
        den = jnp.sum(p, axis=-1)
        o = jnp.einsum('bhqm,bqmhd->bqhd', p, vg.astype(jnp.float32))
        o = o / jnp.transpose(den, (0, 2, 1))[..., None]
        lse = jnp.transpose(m[..., 0] + jnp.log(den), (0, 2, 1))
        return o.astype(q.dtype), lse

    o, lse = lax.map(block, jnp.arange(n_blocks))
    o = jnp.moveaxis(o, 0, 1).reshape(b, s, h, d)
    lse = jnp.moveaxis(lse, 0, 1).reshape(b, s, h)
    return o, lse


def stick_breaking_mixer(x, w_qkv, w_o):
    b, s, _ = x.shape
    qkv = (x @ w_qkv).reshape(b, s, 3, SB_HEADS, HEAD_DIM)
    o = stick_breaking_attention(qkv[:, :, 0], qkv[:, :, 1], qkv[:, :, 2])
    return o.reshape(b, s, SB_HEADS * HEAD_DIM) @ w_o


def dilated_mixer(x, w_qkv, w_o, cos, sin):
    b, s, _ = x.shape
    qkv = (x @ w_qkv).reshape(b, s, 3, N_DIL_GROUPS, DIL_HEADS_PER_GROUP, HEAD_DIM)
    outs, lses = [], []
    for g, (window, dilation) in enumerate(DIL_CONFIGS):
        q = apply_rope(qkv[:, :, 0, g], cos, sin)
        k = apply_rope(qkv[:, :, 1, g], cos, sin)
        o, lse = dilated_window_attention(q, k, qkv[:, :, 2, g], window, dilation)
        outs.append(o)
        lses.append(lse)
    share = jax.nn.softmax(jnp.stack(lses, axis=0), axis=0)
    o = jnp.stack(outs, axis=0) * share[..., None].astype(x.dtype)
    o = jnp.moveaxis(o, 0, 2).reshape(b, s, DIL_HEADS * HEAD_DIM)
    return o @ w_o


def swiglu(x, w_gate_up, w_down):
    gate, up = jnp.split(x @ w_gate_up, 2, axis=-1)
    return (jax.nn.silu(gate) * up) @ w_down


def setup_inputs(seed: int = 0) -> dict:
    key = jax.random.key(seed)
    ks = jax.random.split(key, 12)
    f32 = jnp.float32
    sb_w = SB_HEADS * HEAD_DIM
    dil_w = DIL_HEADS * HEAD_DIM
    sb_col_scale = jnp.concatenate([jnp.ones((2 * sb_w,), f32), jnp.full((sb_w,), DEEPNORM_BETA, f32)])
    dil_col_scale = jnp.concatenate([jnp.ones((2 * dil_w,), f32), jnp.full((dil_w,), DEEPNORM_BETA, f32)])
    x = jax.random.normal(ks[0], (BATCH, SEQ, D_MODEL), f32)
    w_qkv_sb = jax.random.normal(ks[1], (N_SB_LAYERS, D_MODEL, 3 * sb_w), f32) * (D_MODEL ** -0.5) * sb_col_scale
    w_o_sb = jax.random.normal(ks[2], (N_SB_LAYERS, sb_w, D_MODEL), f32) * (sb_w ** -0.5) * DEEPNORM_BETA
    w_qkv_dil = jax.random.normal(ks[3], (N_DIL_LAYERS, D_MODEL, 3 * dil_w), f32) * (D_MODEL ** -0.5) * dil_col_scale
    w_o_dil = jax.random.normal(ks[4], (N_DIL_LAYERS, dil_w, D_MODEL), f32) * (dil_w ** -0.5) * DEEPNORM_BETA
    ln_mix_g = 1.0 + 0.02 * jax.random.normal(ks[5], (DEPTH, D_MODEL), f32)
    ln_mix_b = 0.02 * jax.random.normal(ks[6], (DEPTH, D_MODEL), f32)
    w_gate_up = jax.random.normal(ks[7], (DEPTH, D_MODEL, 2 * D_FF), f32) * (D_MODEL ** -0.5)
    w_down = jax.random.normal(ks[8], (DEPTH, D_FF, D_MODEL), f32) * (D_FF ** -0.5) * DEEPNORM_BETA
    ln_ffn_g = 1.0 + 0.02 * jax.random.normal(ks[9], (DEPTH, D_MODEL), f32)
    ln_ffn_b = 0.02 * jax.random.normal(ks[10], (DEPTH, D_MODEL), f32)
    return {'x': x, 'w_qkv_sb': w_qkv_sb, 'w_o_sb': w_o_sb, 'w_qkv_dil': w_qkv_dil, 'w_o_dil': w_o_dil,
            'ln_mix_g': ln_mix_g, 'ln_mix_b': ln_mix_b, 'w_gate_up': w_gate_up, 'w_down': w_down,
            'ln_ffn_g': ln_ffn_g, 'ln_ffn_b': ln_ffn_b}


def reference(x, w_qkv_sb, w_o_sb, w_qkv_dil, w_o_dil, ln_mix_g, ln_mix_b, w_gate_up, w_down, ln_ffn_g, ln_ffn_b):
    cos, sin = rope_tables(x.shape[1], HEAD_DIM)
    for i in range(DEPTH):
        j = i // 2
        if i % 2 == 0:
            mixed = stick_breaking_mixer(x, w_qkv_sb[j], w_o_sb[j])
        else:
            mixed = dilated_mixer(x, w_qkv_dil[j], w_o_dil[j], cos, sin)
        x = layer_norm(DEEPNORM_ALPHA * x + mixed, ln_mix_g[i], ln_mix_b[i])
        x = layer_norm(DEEPNORM_ALPHA * x + swiglu(x, w_gate_up[i], w_down[i]), ln_ffn_g[i], ln_ffn_b[i])
    return x
```

```python
import functools
import math

import jax
import jax.numpy as jnp
from jax import lax
from jax.experimental import pallas as pl
from jax.experimental.pallas import tpu as pltpu

HEAD_DIM = 128
DIL_CONFIGS = ((128, 1), (512, 4), (2048, 16))
DIL_HEADS_PER_GROUP = 6
ROPE_THETA = 10000.0
LN_EPS = 1e-5

V7X_VMEM_BYTES = 64 * 1024 * 1024
V7X_MXU_DIM = 256
LANES = 128

PROJ_ROWS = 1024
PROJ_COLS = 1024
LN_ROWS = 512
FFN_COLS = 512
FFN_DOWN_K = 512
SB_Q_ROWS = 512
SB_KEY_BLOCK = V7X_MXU_DIM
DIL_Q_ROWS = 256
DIL_SUB = 128

BF16 = jnp.bfloat16
F32 = jnp.float32


def _vmem_limit(*nbytes):
    return int(min(V7X_VMEM_BYTES * 7 // 8, max(2 * sum(nbytes), 16 * 1024 * 1024)))


def _layer_norm_rows(y, g, b):
    mu = jnp.mean(y, axis=-1, keepdims=True)
    d = y - mu
    var = jnp.mean(d * d, axis=-1, keepdims=True)
    return d * lax.rsqrt(var + LN_EPS) * g + b


def _proj_heads_kernel(x_ref, w_ref, o_ref, xb_ref):
    @pl.when(pl.program_id(1) == 0)
    def _():
        xb_ref[...] = x_ref[...].astype(BF16)

    acc = jnp.dot(xb_ref[...], w_ref[...], preferred_element_type=F32)
    for h in range(o_ref.shape[0]):
        o_ref[h] = acc[:, h * HEAD_DIM:(h + 1) * HEAD_DIM].astype(o_ref.dtype)


def _proj_heads(x, w):
    t, k = x.shape
    n = w.shape[1]
    tm, tn = min(PROJ_ROWS, t), min(PROJ_COLS, n)
    hpt = tn // HEAD_DIM
    return pl.pallas_call(
        _proj_heads_kernel,
        grid=(t // tm, n // tn),
        in_specs=[pl.BlockSpec((tm, k), lambda i, j: (i, 0)),
                  pl.BlockSpec((k, tn), lambda i, j: (0, j))],
        out_specs=pl.BlockSpec((hpt, tm, HEAD_DIM), lambda i, j: (j, i, 0)),
        out_shape=jax.ShapeDtypeStruct((n // HEAD_DIM, t, HEAD_DIM), BF16),
        scratch_shapes=[pltpu.VMEM((tm, k), BF16)],
        compiler_params=pltpu.CompilerParams(
            dimension_semantics=("parallel", "arbitrary"),
            vmem_limit_bytes=_vmem_limit(2 * tm * k * 4, tm * k * 2, 2 * k * tn * 2, 2 * tm * tn * 2, tm * tn * 4)),
        name="sb_qkv_proj",
    )(x, w)


def _sb_attn_kernel(q_ref, k_ref, v_ref, tri_ref, o_ref, acc_ref, run_ref, *, scale):
    tq = q_ref.shape[0]
    tk = SB_KEY_BLOCK
    diag_blocks = tq // tk
    i = pl.program_id(2)
    acc_ref[...] = jnp.zeros_like(acc_ref)
    run_ref[...] = jnp.zeros_like(run_ref)

    def visit(kb, row0, masked):
        k0 = pl.multiple_of(kb * tk, tk)
        q = q_ref[row0:, :]
        kblk = k_ref[pl.ds(k0, tk), :]
        vblk = v_ref[pl.ds(k0, tk), :]
        z = lax.dot_general(q, kblk, (((1,), (1,)), ((), ())), preferred_element_type=F32) * scale
        log_stay = jnp.minimum(-z, 0.0) - jnp.log(1.0 + jnp.exp(-jnp.abs(z)))
        if masked:
            qpos = i * tq + row0 + lax.broadcasted_iota(jnp.int32, z.shape, 0)
            kpos = k0 + lax.broadcasted_iota(jnp.int32, z.shape, 1)
            causal = kpos < qpos
            log_stay = jnp.where(causal, log_stay, 0.0)
        hi = log_stay.astype(BF16)
        lo = (log_stay - hi.astype(F32)).astype(BF16)
        later = jnp.dot(jnp.concatenate([hi, lo], axis=1), tri_ref[...], preferred_element_type=F32)
        w = jnp.exp(log_stay + z + later + run_ref[row0:, :])
        if masked:
            w = jnp.where(causal, w, 0.0)
        acc_ref[row0:, :] += jnp.dot(w.astype(BF16), vblk, preferred_element_type=F32)
        run_ref[row0:, :] += jnp.sum(log_stay, axis=1, keepdims=True)

    for d in reversed(range(diag_blocks)):
        visit(i * diag_blocks + d, d * tk, True)

    def body(n, carry):
        visit(i * diag_blocks - 1 - n, 0, False)
        return carry

    lax.fori_loop(0, i * diag_blocks, body, 0)
    o_ref[...] = acc_ref[...].astype(o_ref.dtype)


def _sb_attention(qkv, batch, seq, n_heads):
    tq, tk = min(SB_Q_ROWS, seq), SB_KEY_BLOCK
    nq = seq // tq
    tri = (lax.broadcasted_iota(jnp.int32, (tk, tk), 0) > lax.broadcasted_iota(jnp.int32, (tk, tk), 1))
    tri2 = jnp.concatenate([tri, tri], axis=0).astype(BF16)
    return pl.pallas_call(
        functools.partial(_sb_attn_kernel, scale=1.0 / math.sqrt(HEAD_DIM)),
        grid=(n_heads, batch, nq),
        in_specs=[pl.BlockSpec((None, tq, HEAD_DIM), lambda h, b, i: (h, b * nq + i, 0)),
                  pl.BlockSpec((None, seq, HEAD_DIM), lambda h, b, i: (n_heads + h, b, 0)),
                  pl.BlockSpec((None, seq, HEAD_DIM), lambda h, b, i: (2 * n_heads + h, b, 0)),
                  pl.BlockSpec((2 * tk, tk), lambda h, b, i: (0, 0))],
        out_specs=pl.BlockSpec((None, tq, HEAD_DIM), lambda h, b, i: (h, b * nq + i, 0)),
        out_shape=jax.ShapeDtypeStruct((n_heads, batch * seq, HEAD_DIM), BF16),
        scratch_shapes=[pltpu.VMEM((tq, HEAD_DIM), F32), pltpu.VMEM((tq, 1), F32)],
        compiler_params=pltpu.CompilerParams(
            dimension_semantics=("parallel", "parallel", "arbitrary"),
            vmem_limit_bytes=_vmem_limit(4 * seq * HEAD_DIM * 2, 8 * tq * tk * 4)),
        name="sb_attention",
    )(qkv, qkv, qkv, tri2)


def _outproj_ln_kernel(o_ref, w_ref, x_ref, g_ref, b_ref, y_ref, *, alpha):
    lhs = jnp.concatenate([o_ref[h] for h in range(o_ref.shape[0])], axis=1)
    mixed = jnp.dot(lhs, w_ref[...], preferred_element_type=F32)
    y_ref[...] = _layer_norm_rows(alpha * x_ref[...] + mixed, g_ref[...], b_ref[...])


def _outproj_ln(o, w, x, g, b, alpha):
    n_heads, t, _ = o.shape
    k, d = w.shape
    tm = min(LN_ROWS, t)
    return pl.pallas_call(
        functools.partial(_outproj_ln_kernel, alpha=alpha),
        grid=(t // tm,),
        in_specs=[pl.BlockSpec((n_heads, tm, HEAD_DIM), lambda i: (0, i, 0)),
                  pl.BlockSpec((k, d), lambda i: (0, 0)),
                  pl.BlockSpec((tm, d), lambda i: (i, 0)),
                  pl.BlockSpec((1, d), lambda i: (0, 0)),
                  pl.BlockSpec((1, d), lambda i: (0, 0))],
        out_specs=pl.BlockSpec((tm, d), lambda i: (i, 0)),
        out_shape=jax.ShapeDtypeStruct((t, d), F32),
        compiler_params=pltpu.CompilerParams(
            dimension_semantics=("parallel",),
            vmem_limit_bytes=_vmem_limit(2 * tm * k * 2, 2 * k * d * 2, 4 * tm * d * 4, 2 * tm * d * 4)),
        name="sb_outproj_ln",
    )(o, w, x, g, b)


def _ffn_up_kernel(x_ref, wg_ref, wu_ref, h_ref, xb_ref):
    @pl.when(pl.program_id(1) == 0)
    def _():
        xb_ref[...] = x_ref[...].astype(BF16)

    xb = xb_ref[...]
    gate = jnp.dot(xb, wg_ref[...], preferred_element_type=F32)
    up = jnp.dot(xb, wu_ref[...], preferred_element_type=F32)
    h_ref[...] = (jax.nn.silu(gate) * up).astype(h_ref.dtype)


def _ffn_up(x, w_gate_up):
    t, d = x.shape
    f = w_gate_up.shape[1] // 2
    tm, tn = min(PROJ_ROWS, t), min(FFN_COLS, f)
    nj = f // tn
    return pl.pallas_call(
        _ffn_up_kernel,
        grid=(t // tm, nj),
        in_specs=[pl.BlockSpec((tm, d), lambda i, j: (i, 0)),
                  pl.BlockSpec((d, tn), lambda i, j: (0, j)),
                  pl.BlockSpec((d, tn), lambda i, j: (0, nj + j))],
        out_specs=pl.BlockSpec((tm, tn), lambda i, j: (i, j)),
        out_shape=jax.ShapeDtypeStruct((t, f), BF16),
        scratch_shapes=[pltpu.VMEM((tm, d), BF16)],
        compiler_params=pltpu.CompilerParams(
            dimension_semantics=("parallel", "arbitrary"),
            vmem_limit_bytes=_vmem_limit(2 * tm * d * 4, tm * d * 2, 4 * d * tn * 2, 2 * tm * tn * 2, 3 * tm * tn * 4)),
        name="ffn_up",
    )(x, w_gate_up, w_gate_up)


def _ffn_down_ln_kernel(h_ref, w_ref, x_ref, g_ref, b_ref, y_ref, acc_ref, *, alpha):
    kk = pl.program_id(1)

    @pl.when(kk == 0)
    def _():
        acc_ref[...] = jnp.zeros_like(acc_ref)

    acc_ref[...] += jnp.dot(h_ref[...], w_ref[...], preferred_element_type=F32)

    @pl.when(kk == pl.num_programs(1) - 1)
    def _():
        y_ref[...] = _layer_norm_rows(alpha * x_ref[...] + acc_ref[...], g_ref[...], b_ref[...])


def _ffn_down_ln(h, w, x, g, b, alpha):
    t, f = h.shape
    d = w.shape[1]
    tm, tk = min(LN_ROWS, t), min(FFN_DOWN_K, f)
    return pl.pallas_call(
        functools.partial(_ffn_down_ln_kernel, alpha=alpha),
        grid=(t // tm, f // tk),
        in_specs=[pl.BlockSpec((tm, tk), lambda i, k: (i, k)),
                  pl.BlockSpec((tk, d), lambda i, k: (k, 0)),
                  pl.BlockSpec((tm, d), lambda i, k: (i, 0)),
                  pl.BlockSpec((1, d), lambda i, k: (0, 0)),
                  pl.BlockSpec((1, d), lambda i, k: (0, 0))],
        out_specs=pl.BlockSpec((tm, d), lambda i, k: (i, 0)),
        out_shape=jax.ShapeDtypeStruct((t, d), F32),
        scratch_shapes=[pltpu.VMEM((tm, d), F32)],
        compiler_params=pltpu.CompilerParams(
            dimension_semantics=("parallel", "arbitrary"),
            vmem_limit_bytes=_vmem_limit(2 * tm * tk * 2, 2 * tk * d * 2, 5 * tm * d * 4)),
        name="ffn_down_ln",
    )(h, w, x, g, b)


def _rope_tables(seq):
    pos = jnp.arange(seq, dtype=F32)
    inv_freq = ROPE_THETA ** (-jnp.arange(0, HEAD_DIM, 2, dtype=F32) / HEAD_DIM)
    ang = pos[:, None] * inv_freq[None, :]
    cos, sin = jnp.cos(ang), jnp.sin(ang)
    return jnp.concatenate([cos, cos], axis=1), jnp.concatenate([-sin, sin], axis=1)


def _dil_proj_kernel(x_ref, wq_ref, wk_ref, wv_ref, cos_ref, sin_ref, q_ref, k_ref, v_ref):
    xb = x_ref[...].astype(BF16)
    cos, sin = cos_ref[...], sin_ref[...]
    for w_ref, out_ref, rope in ((wq_ref, q_ref, True), (wk_ref, k_ref, True), (wv_ref, v_ref, False)):
        acc = jnp.dot(xb, w_ref[...], preferred_element_type=F32)
        for h in range(out_ref.shape[0]):
            a = acc[:, h * HEAD_DIM:(h + 1) * HEAD_DIM]
            if rope:
                a = a * cos + pltpu.roll(a, HEAD_DIM // 2, 1) * sin
            out_ref[h] = a.astype(out_ref.dtype)


def _dil_proj(x, w, group, dilation, cos2, sin2):
    b, s, d = x.shape
    r = dilation
    length = s // r
    tu = min(PROJ_ROWS, length)
    n_groups = len(DIL_CONFIGS)
    gw = DIL_HEADS_PER_GROUP * HEAD_DIM
    xv = x.reshape(b, length, r * d)
    cos_r = cos2.reshape(length, r, HEAD_DIM).transpose(1, 0, 2)
    sin_r = sin2.reshape(length, r, HEAD_DIM).transpose(1, 0, 2)
    w_spec = lambda kind: pl.BlockSpec((d, gw), lambda bi, c, iu: (0, kind * n_groups + group))
    tab_spec = pl.BlockSpec((None, tu, HEAD_DIM), lambda bi, c, iu: (c, iu, 0))
    out_spec = pl.BlockSpec((DIL_HEADS_PER_GROUP, None, None, tu, HEAD_DIM), lambda bi, c, iu: (0, bi, c, iu, 0))
    out_sds = jax.ShapeDtypeStruct((DIL_HEADS_PER_GROUP, b, r, length, HEAD_DIM), BF16)
    return pl.pallas_call(
        _dil_proj_kernel,
        grid=(b, r, length // tu),
        in_specs=[pl.BlockSpec((None, tu, d), lambda bi, c, iu: (bi, iu, c)),
                  w_spec(0), w_spec(1), w_spec(2), tab_spec, tab_spec],
        out_specs=[out_spec, out_spec, out_spec],
        out_shape=[out_sds, out_sds, out_sds],
        compiler_params=pltpu.CompilerParams(
            dimension_semantics=("parallel", "parallel", "parallel"),
            vmem_limit_bytes=_vmem_limit(2 * tu * d * 4, tu * d * 2, 6 * d * gw * 2, 6 * tu * gw * 2, 2 * tu * gw * 4)),
        name=f"dil_qkv_proj_r{r}",
    )(xv, w, w, w, cos_r, sin_r)


def _dil_attn_kernel(q_ref, k_ref, v_ref, kp_ref, vp_ref, o_ref, lse_ref, *, scale):
    n_heads, tu, _ = q_ref.shape
    sub = DIL_SUB
    iu = pl.program_id(2)
    qi = lax.broadcasted_iota(jnp.int32, (sub, 2 * sub), 0)
    kj = lax.broadcasted_iota(jnp.int32, (sub, 2 * sub), 1)
    in_window = (kj >= qi) & (kj <= qi + sub)
    lane = lax.broadcasted_iota(jnp.int32, (sub, LANES), 1)
    for s in range(tu // sub):
        rows = slice(s * sub, (s + 1) * sub)
        if s == 0:
            valid = in_window & (kj >= jnp.where(iu > 0, 0, sub))
        else:
            valid = in_window
        lse_tile = jnp.zeros((sub, LANES), F32)
        for h in range(n_heads):
            q = q_ref[h, rows, :]
            if s == 0:
                k_prev, v_prev = kp_ref[h], vp_ref[h]
            else:
                k_prev, v_prev = k_ref[h, (s - 1) * sub:s * sub, :], v_ref[h, (s - 1) * sub:s * sub, :]
            keys = jnp.concatenate([k_prev, k_ref[h, rows, :]], axis=0)
            vals = jnp.concatenate([v_prev, v_ref[h, rows, :]], axis=0)
            sc = lax.dot_general(q, keys, (((1,), (1,)), ((), ())), preferred_element_type=F32) * scale
            sc = jnp.where(valid, sc, -jnp.inf)
            m = jnp.max(sc, axis=1, keepdims=True)
            p = jnp.exp(sc - m)
            den = jnp.sum(p, axis=1, keepdims=True)
            o = jnp.dot(p.astype(BF16), vals, preferred_element_type=F32) / den
            o_ref[rows, h * HEAD_DIM:(h + 1) * HEAD_DIM] = o.astype(o_ref.dtype)
            lse_tile = jnp.where(lane == h, m + jnp.log(den), lse_tile)
        lse_ref[rows, :] = lse_tile


def _dil_attention(q, k, v):
    n_heads, b, r, length, _ = q.shape
    tu = min(DIL_Q_ROWS, length)
    sub = DIL_SUB
    gw = n_heads * HEAD_DIM
    cur = pl.BlockSpec((n_heads, None, None, tu, HEAD_DIM), lambda bi, c, iu: (0, bi, c, iu, 0))
    prev = pl.BlockSpec((n_heads, None, None, sub, HEAD_DIM),
                        lambda bi, c, iu: (0, bi, c, jnp.maximum(iu * (tu // sub) - 1, 0), 0))
    o, lse = pl.pallas_call(
        functools.partial(_dil_attn_kernel, scale=1.0 / math.sqrt(HEAD_DIM)),
        grid=(b, r, length // tu),
        in_specs=[cur, cur, cur, prev, prev],
        out_specs=[pl.BlockSpec((None, tu, gw), lambda bi, c, iu: (bi, iu, c)),
                   pl.BlockSpec((None, tu, LANES), lambda bi, c, iu: (bi, iu, c))],
        out_shape=[jax.ShapeDtypeStruct((b, length, r * gw), BF16),
                   jax.ShapeDtypeStruct((b, length, r * LANES), F32)],
        compiler_params=pltpu.CompilerParams(
            dimension_semantics=("parallel", "parallel", "parallel"),
            vmem_limit_bytes=_vmem_limit(6 * tu * gw * 2, 4 * sub * gw * 2, 2 * tu * gw * 2, 2 * tu * LANES * 4)),
        name=f"dil_attention_r{r}",
    )(q, k, v, k, v)
    return o.reshape(b, length * r, gw), lse.reshape(b, length * r, LANES)


def _dil_outproj_ln_kernel(o0_ref, o1_ref, o2_ref, l0_ref, l1_ref, l2_ref, w_ref, x_ref, g_ref, b_ref, y_ref, *, alpha):
    lses = [l0_ref[...], l1_ref[...], l2_ref[...]]
    mx = jnp.maximum(jnp.maximum(lses[0], lses[1]), lses[2])
    es = [jnp.exp(l - mx) for l in lses]
    den = es[0] + es[1] + es[2]
    pieces = []
    for o_ref, e in zip((o0_ref, o1_ref, o2_ref), es):
        share = e / den
        for h in range(DIL_HEADS_PER_GROUP):
            o = o_ref[:, h * HEAD_DIM:(h + 1) * HEAD_DIM].astype(F32)
            pieces.append((o * share[:, h:h + 1]).astype(BF16))
    lhs = jnp.concatenate(pieces, axis=1)
    mixed = jnp.dot(lhs, w_ref[...], preferred_element_type=F32)
    y_ref[...] = _layer_norm_rows(alpha * x_ref[...] + mixed, g_ref[...], b_ref[...])


def _dil_outproj_ln(os, lses, w, x, g, b, alpha):
    t, gw = os[0].shape
    k, d = w.shape
    tm = min(LN_ROWS, t)
    o_spec = pl.BlockSpec((tm, gw), lambda i: (i, 0))
    l_spec = pl.BlockSpec((tm, LANES), lambda i: (i, 0))
    return pl.pallas_call(
        functools.partial(_dil_outproj_ln_kernel, alpha=alpha),
        grid=(t // tm,),
        in_specs=[o_spec, o_spec, o_spec, l_spec, l_spec, l_spec,
                  pl.BlockSpec((k, d), lambda i: (0, 0)),
                  pl.BlockSpec((tm, d), lambda i: (i, 0)),
                  pl.BlockSpec((1, d), lambda i: (0, 0)),
                  pl.BlockSpec((1, d), lambda i: (0, 0))],
        out_specs=pl.BlockSpec((tm, d), lambda i: (i, 0)),
        out_shape=jax.ShapeDtypeStruct((t, d), F32),
        compiler_params=pltpu.CompilerParams(
            dimension_semantics=("parallel",),
            vmem_limit_bytes=_vmem_limit(3 * tm * k * 2, 2 * k * d * 2, 4 * tm * d * 4, 2 * tm * d * 4)),
        name="dil_outproj_ln",
    )(*os, *lses, w, x, g, b)


def kernel(x, w_qkv_sb, w_o_sb, w_qkv_dil, w_o_dil, ln_mix_g, ln_mix_b, w_gate_up, w_down, ln_ffn_g, ln_ffn_b):
    batch, seq, d_model = x.shape
    depth = ln_mix_g.shape[0]
    alpha = (2.0 * depth) ** 0.25
    t = batch * seq
    sb_heads = w_o_sb.shape[1] // HEAD_DIM
    assert all(window // dilation == DIL_SUB for window, dilation in DIL_CONFIGS)

    w_qkv_sb, w_o_sb, w_qkv_dil, w_o_dil, w_gate_up, w_down = (
        w.astype(BF16) for w in (w_qkv_sb, w_o_sb, w_qkv_dil, w_o_dil, w_gate_up, w_down))
    cos2, sin2 = _rope_tables(seq)

    xf = x.reshape(t, d_model)
    for layer in range(depth):
        j = layer // 2
        g_mix, b_mix = ln_mix_g[layer][None, :], ln_mix_b[layer][None, :]
        if layer % 2 == 0:
            qkv = _proj_heads(xf, w_qkv_sb[j])
            o = _sb_attention(qkv, batch, seq, sb_heads)
            xf = _outproj_ln(o, w_o_sb[j], xf, g_mix, b_mix, alpha)
        else:
            os, lses = [], []
            for group, (_, dilation) in enumerate(DIL_CONFIGS):
                q, k, v = _dil_proj(xf.reshape(batch, seq, d_model), w_qkv_dil[j], group, dilation, cos2, sin2)
                o, lse = _dil_attention(q, k, v)
                os.append(o.reshape(t, -1))
                lses.append(lse.reshape(t, LANES))
            xf = _dil_outproj_ln(os, lses, w_o_dil[j], xf, g_mix, b_mix, alpha)
        h = _ffn_up(xf, w_gate_up[layer])
        xf = _ffn_down_ln(h, w_down[layer], xf, ln_ffn_g[layer][None, :], ln_ffn_b[layer][None, :], alpha)
    return xf.reshape(batch, seq, d_model)
```

```python
import functools
import math

import jax
import jax.numpy as jnp
from jax import lax
from jax.experimental import pallas as pl
from jax.experimental.pallas import tpu as pltpu

HEAD_DIM = 128
DIL_CONFIGS = ((128, 1), (512, 4), (2048, 16))
DIL_HEADS_PER_GROUP = 6
ROPE_THETA = 10000.0
LN_EPS = 1e-5
LOG2_E = math.log2(math.e)

V7X_VMEM_BYTES = 64 * 1024 * 1024
V7X_MXU_DIM = 256
LANES = 128

PROJ_ROWS = 1024
PROJ_COLS = 1024
LN_ROWS = 512
FFN_COLS = 512
FFN_DOWN_K = 512
SB_Q_ROWS = 512
SB_KEY_BLOCK = V7X_MXU_DIM
SB_HEADS_PER_STEP = 2
DIL_Q_ROWS = 256
DIL_SUB = 128

BF16 = jnp.bfloat16
F32 = jnp.float32


def _vmem_limit(*nbytes):
    return int(min(V7X_VMEM_BYTES * 7 // 8, max(2 * sum(nbytes), 16 * 1024 * 1024)))


def _layer_norm_rows(y, g, b):
    mu = jnp.mean(y, axis=-1, keepdims=True)
    d = y - mu
    var = jnp.mean(d * d, axis=-1, keepdims=True)
    return d * lax.rsqrt(var + LN_EPS) * g + b


def _proj_heads_kernel(x_ref, w_ref, o_ref, xb_ref):
    @pl.when(pl.program_id(1) == 0)
    def _():
        xb_ref[...] = x_ref[...].astype(BF16)

    acc = jnp.dot(xb_ref[...], w_ref[...], preferred_element_type=F32)
    for h in range(o_ref.shape[0]):
        o_ref[h] = acc[:, h * HEAD_DIM:(h + 1) * HEAD_DIM].astype(o_ref.dtype)


def _proj_heads(x, w):
    t, k = x.shape
    n = w.shape[1]
    tm, tn = min(PROJ_ROWS, t), min(PROJ_COLS, n)
    hpt = tn // HEAD_DIM
    return pl.pallas_call(
        _proj_heads_kernel,
        grid=(t // tm, n // tn),
        in_specs=[pl.BlockSpec((tm, k), lambda i, j: (i, 0)),
                  pl.BlockSpec((k, tn), lambda i, j: (0, j))],
        out_specs=pl.BlockSpec((hpt, tm, HEAD_DIM), lambda i, j: (j, i, 0)),
        out_shape=jax.ShapeDtypeStruct((n // HEAD_DIM, t, HEAD_DIM), BF16),
        scratch_shapes=[pltpu.VMEM((tm, k), BF16)],
        compiler_params=pltpu.CompilerParams(
            dimension_semantics=("parallel", "arbitrary"),
            vmem_limit_bytes=_vmem_limit(2 * tm * k * 4, tm * k * 2, 2 * k * tn * 2, 2 * tm * tn * 2, tm * tn * 4)),
        name="sb_qkv_proj",
    )(x, w)


def _sb_attn_kernel(q_ref, k_ref, v_ref, tri_ref, o_ref, acc_ref, run_ref, *, scale):
    n_heads, tq, _ = q_ref.shape
    tk = SB_KEY_BLOCK
    diag_blocks = tq // tk
    i = pl.program_id(2)
    acc_ref[...] = jnp.zeros_like(acc_ref)
    run_ref[...] = jnp.zeros_like(run_ref)

    def visit(k0, row0, n_blocks, masked):
        heads = range(n_heads)
        zs, his, los = [], [], []
        for h in heads:
            q = q_ref[h, row0:, :]
            kblk = k_ref[h, pl.ds(k0, n_blocks * tk), :]
            zs.append(lax.dot_general(q, kblk, (((1,), (1,)), ((), ())), preferred_element_type=F32) * scale)
        if masked:
            qpos = i * tq + row0 + lax.broadcasted_iota(jnp.int32, zs[0].shape, 0)
            kpos = k0 + lax.broadcasted_iota(jnp.int32, zs[0].shape, 1)
            causal = kpos < qpos
        for z in zs:
            cost = jnp.maximum(z, 0.0) + jnp.log(1.0 + jnp.exp2(jnp.abs(z) * (-LOG2_E)))
            if masked:
                cost = jnp.where(causal, cost, 0.0)
            hi = cost.astype(BF16)
            his.append(hi)
            los.append((cost - hi.astype(F32)).astype(BF16))
        runs = [run_ref[h, row0:, :] for h in heads]
        pieces = [[None] * n_blocks for _ in heads]
        for s in reversed(range(n_blocks)):
            cols = slice(s * tk, (s + 1) * tk)
            for h in heads:
                incl = jnp.dot(jnp.concatenate([his[h][:, cols], los[h][:, cols]], axis=1), tri_ref[...],
                               preferred_element_type=F32)
                pieces[h][s] = jnp.exp(zs[h][:, cols] - incl - runs[h])
                runs[h] = runs[h] + incl[:, 0:1]
        for h in heads:
            w = pieces[h][0] if n_blocks == 1 else jnp.concatenate(pieces[h], axis=1)
            if masked:
                w = jnp.where(causal, w, 0.0)
            vblk = v_ref[h, pl.ds(k0, n_blocks * tk), :]
            acc_ref[h, row0:, :] += jnp.dot(w.astype(BF16), vblk, preferred_element_type=F32)
            run_ref[h, row0:, :] = runs[h]

    diag0 = pl.multiple_of(i * tq, tq)
    for d in reversed(range(diag_blocks)):
        visit(diag0 + d * tk, d * tk, 1, True)

    def body(n, carry):
        visit(pl.multiple_of((i - 1 - n) * tq, tq), 0, diag_blocks, False)
        return carry

    lax.fori_loop(0, i, body, 0)
    o_ref[...] = acc_ref[...].astype(o_ref.dtype)


def _sb_attention(qkv, batch, seq, n_heads):
    tq, tk = min(SB_Q_ROWS, seq), SB_KEY_BLOCK
    nq = seq // tq
    hp = SB_HEADS_PER_STEP
    ng = n_heads // hp
    tri = (lax.broadcasted_iota(jnp.int32, (tk, tk), 0) >= lax.broadcasted_iota(jnp.int32, (tk, tk), 1))
    tri2 = jnp.concatenate([tri, tri], axis=0).astype(BF16)
    return pl.pallas_call(
        functools.partial(_sb_attn_kernel, scale=1.0 / math.sqrt(HEAD_DIM)),
        grid=(ng, batch, nq),
        in_specs=[pl.BlockSpec((hp, tq, HEAD_DIM), lambda g, b, i: (g, b * nq + i, 0)),
                  pl.BlockSpec((hp, seq, HEAD_DIM), lambda g, b, i: (ng + g, b, 0)),
                  pl.BlockSpec((hp, seq, HEAD_DIM), lambda g, b, i: (2 * ng + g, b, 0)),
                  pl.BlockSpec((2 * tk, tk), lambda g, b, i: (0, 0))],
        out_specs=pl.BlockSpec((hp, tq, HEAD_DIM), lambda g, b, i: (g, b * nq + i, 0)),
        out_shape=jax.ShapeDtypeStruct((n_heads, batch * seq, HEAD_DIM), BF16),
        scratch_shapes=[pltpu.VMEM((hp, tq, HEAD_DIM), F32), pltpu.VMEM((hp, tq, 1), F32)],
        compiler_params=pltpu.CompilerParams(
            dimension_semantics=("parallel", "parallel", "arbitrary"),
            vmem_limit_bytes=_vmem_limit(4 * hp * seq * HEAD_DIM * 2, 8 * hp * tq * tq * 4)),
        name="sb_attention",
    )(qkv, qkv, qkv, tri2)


def _outproj_ln_kernel(o_ref, w_ref, x_ref, g_ref, b_ref, y_ref, *, alpha):
    lhs = jnp.concatenate([o_ref[h] for h in range(o_ref.shape[0])], axis=1)
    mixed = jnp.dot(lhs, w_ref[...], preferred_element_type=F32)
    y_ref[...] = _layer_norm_rows(alpha * x_ref[...] + mixed, g_ref[...], b_ref[...])


def _outproj_ln(o, w, x, g, b, alpha):
    n_heads, t, _ = o.shape
    k, d = w.shape
    tm = min(LN_ROWS, t)
    return pl.pallas_call(
        functools.partial(_outproj_ln_kernel, alpha=alpha),
        grid=(t // tm,),
        in_specs=[pl.BlockSpec((n_heads, tm, HEAD_DIM), lambda i: (0, i, 0)),
                  pl.BlockSpec((k, d), lambda i: (0, 0)),
                  pl.BlockSpec((tm, d), lambda i: (i, 0)),
                  pl.BlockSpec((1, d), lambda i: (0, 0)),
                  pl.BlockSpec((1, d), lambda i: (0, 0))],
        out_specs=pl.BlockSpec((tm, d), lambda i: (i, 0)),
        out_shape=jax.ShapeDtypeStruct((t, d), F32),
        compiler_params=pltpu.CompilerParams(
            dimension_semantics=("parallel",),
            vmem_limit_bytes=_vmem_limit(2 * tm * k * 2, 2 * k * d * 2, 4 * tm * d * 4, 2 * tm * d * 4)),
        name="sb_outproj_ln",
    )(o, w, x, g, b)


def _ffn_up_kernel(x_ref, wg_ref, wu_ref, h_ref, xb_ref):
    @pl.when(pl.program_id(1) == 0)
    def _():
        xb_ref[...] = x_ref[...].astype(BF16)

    xb = xb_ref[...]
    gate = jnp.dot(xb, wg_ref[...], preferred_element_type=F32)
    up = jnp.dot(xb, wu_ref[...], preferred_element_type=F32)
    h_ref[...] = (jax.nn.silu(gate) * up).astype(h_ref.dtype)


def _ffn_up(x, w_gate_up):
    t, d = x.shape
    f = w_gate_up.shape[1] // 2
    tm, tn = min(PROJ_ROWS, t), min(FFN_COLS, f)
    nj = f // tn
    return pl.pallas_call(
        _ffn_up_kernel,
        grid=(t // tm, nj),
        in_specs=[pl.BlockSpec((tm, d), lambda i, j: (i, 0)),
                  pl.BlockSpec((d, tn), lambda i, j: (0, j)),
                  pl.BlockSpec((d, tn), lambda i, j: (0, nj + j))],
        out_specs=pl.BlockSpec((tm, tn), lambda i, j: (i, j)),
        out_shape=jax.ShapeDtypeStruct((t, f), BF16),
        scratch_shapes=[pltpu.VMEM((tm, d), BF16)],
        compiler_params=pltpu.CompilerParams(
            dimension_semantics=("parallel", "arbitrary"),
            vmem_limit_bytes=_vmem_limit(2 * tm * d * 4, tm * d * 2, 4 * d * tn * 2, 2 * tm * tn * 2, 3 * tm * tn * 4)),
        name="ffn_up",
    )(x, w_gate_up, w_gate_up)


def _ffn_down_ln_kernel(h_ref, w_ref, x_ref, g_ref, b_ref, y_ref, acc_ref, *, alpha):
    kk = pl.program_id(1)

    @pl.when(kk == 0)
    def _():
        acc_ref[...] = jnp.zeros_like(acc_ref)

    acc_ref[...] += jnp.dot(h_ref[...], w_ref[...], preferred_element_type=F32)

    @pl.when(kk == pl.num_programs(1) - 1)
    def _():
        y_ref[...] = _layer_norm_rows(alpha * x_ref[...] + acc_ref[...], g_ref[...], b_ref[...])


def _ffn_down_ln(h, w, x, g, b, alpha):
    t, f = h.shape
    d = w.shape[1]
    tm, tk = min(LN_ROWS, t), min(FFN_DOWN_K, f)
    return pl.pallas_call(
        functools.partial(_ffn_down_ln_kernel, alpha=alpha),
        grid=(t // tm, f // tk),
        in_specs=[pl.BlockSpec((tm, tk), lambda i, k: (i, k)),
                  pl.BlockSpec((tk, d), lambda i, k: (k, 0)),
                  pl.BlockSpec((tm, d), lambda i, k: (i, 0)),
                  pl.BlockSpec((1, d), lambda i, k: (0, 0)),
                  pl.BlockSpec((1, d), lambda i, k: (0, 0))],
        out_specs=pl.BlockSpec((tm, d), lambda i, k: (i, 0)),
        out_shape=jax.ShapeDtypeStruct((t, d), F32),
        scratch_shapes=[pltpu.VMEM((tm, d), F32)],
        compiler_params=pltpu.CompilerParams(
            dimension_semantics=("parallel", "arbitrary"),
            vmem_limit_bytes=_vmem_limit(2 * tm * tk * 2, 2 * tk * d * 2, 5 * tm * d * 4)),
        name="ffn_down_ln",
    )(h, w, x, g, b)


def _rope_tables(seq):
    pos = jnp.arange(seq, dtype=F32)
    inv_freq = ROPE_THETA ** (-jnp.arange(0, HEAD_DIM, 2, dtype=F32) / HEAD_DIM)
    ang = pos[:, None] * inv_freq[None, :]
    cos, sin = jnp.cos(ang), jnp.sin(ang)
    return jnp.concatenate([cos, cos], axis=1), jnp.concatenate([-sin, sin], axis=1)


def _dil_proj_kernel(x_ref, wq_ref, wk_ref, wv_ref, cos_ref, sin_ref, q_ref, k_ref, v_ref):
    xb = x_ref[...].astype(BF16)
    cos, sin = cos_ref[...], sin_ref[...]
    for w_ref, out_ref, rope in ((wq_ref, q_ref, True), (wk_ref, k_ref, True), (wv_ref, v_ref, False)):
        acc = jnp.dot(xb, w_ref[...], preferred_element_type=F32)
        for h in range(out_ref.shape[0]):
            a = acc[:, h * HEAD_DIM:(h + 1) * HEAD_DIM]
            if rope:
                a = a * cos + pltpu.roll(a, HEAD_DIM // 2, 1) * sin
            out_ref[h] = a.astype(out_ref.dtype)


def _dil_proj(x, w, group, dilation, cos2, sin2):
    b, s, d = x.shape
    r = dilation
    length = s // r
    tu = min(PROJ_ROWS, length)
    n_groups = len(DIL_CONFIGS)
    gw = DIL_HEADS_PER_GROUP * HEAD_DIM
    xv = x.reshape(b, length, r * d)
    cos_r = cos2.reshape(length, r, HEAD_DIM).transpose(1, 0, 2)
    sin_r = sin2.reshape(length, r, HEAD_DIM).transpose(1, 0, 2)
    w_spec = lambda kind: pl.BlockSpec((d, gw), lambda bi, c, iu: (0, kind * n_groups + group))
    tab_spec = pl.BlockSpec((None, tu, HEAD_DIM), lambda bi, c, iu: (c, iu, 0))
    out_spec = pl.BlockSpec((DIL_HEADS_PER_GROUP, None, None, tu, HEAD_DIM), lambda bi, c, iu: (0, bi, c, iu, 0))
    out_sds = jax.ShapeDtypeStruct((DIL_HEADS_PER_GROUP, b, r, length, HEAD_DIM), BF16)
    return pl.pallas_call(
        _dil_proj_kernel,
        grid=(b, r, length // tu),
        in_specs=[pl.BlockSpec((None, tu, d), lambda bi, c, iu: (bi, iu, c)),
                  w_spec(0), w_spec(1), w_spec(2), tab_spec, tab_spec],
        out_specs=[out_spec, out_spec, out_spec],
        out_shape=[out_sds, out_sds, out_sds],
        compiler_params=pltpu.CompilerParams(
            dimension_semantics=("parallel", "parallel", "parallel"),
            vmem_limit_bytes=_vmem_limit(2 * tu * d * 4, tu * d * 2, 6 * d * gw * 2, 6 * tu * gw * 2, 2 * tu * gw * 4)),
        name=f"dil_qkv_proj_r{r}",
    )(xv, w, w, w, cos_r, sin_r)


def _dil_attn_kernel(q_ref, k_ref, v_ref, kp_ref, vp_ref, o_ref, lse_ref, *, scale):
    n_heads, tu, _ = q_ref.shape
    sub = DIL_SUB
    iu = pl.program_id(2)
    qi = lax.broadcasted_iota(jnp.int32, (sub, 2 * sub), 0)
    kj = lax.broadcasted_iota(jnp.int32, (sub, 2 * sub), 1)
    in_window = (kj >= qi) & (kj <= qi + sub)
    lane = lax.broadcasted_iota(jnp.int32, (sub, LANES), 1)
    for s in range(tu // sub):
        rows = slice(s * sub, (s + 1) * sub)
        if s == 0:
            valid = in_window & (kj >= jnp.where(iu > 0, 0, sub))
        else:
            valid = in_window
        lse_tile = jnp.zeros((sub, LANES), F32)
        for h in range(n_heads):
            q = q_ref[h, rows, :]
            if s == 0:
                k_prev, v_prev = kp_ref[h], vp_ref[h]
            else:
                k_prev, v_prev = k_ref[h, (s - 1) * sub:s * sub, :], v_ref[h, (s - 1) * sub:s * sub, :]
            keys = jnp.concatenate([k_prev, k_ref[h, rows, :]], axis=0)
            vals = jnp.concatenate([v_prev, v_ref[h, rows, :]], axis=0)
            sc = lax.dot_general(q, keys, (((1,), (1,)), ((), ())), preferred_element_type=F32) * scale
            sc = jnp.where(valid, sc, -jnp.inf)
            m = jnp.max(sc, axis=1, keepdims=True)
            p = jnp.exp(sc - m)
            den = jnp.sum(p, axis=1, keepdims=True)
            o = jnp.dot(p.astype(BF16), vals, preferred_element_type=F32) / den
            o_ref[rows, h * HEAD_DIM:(h + 1) * HEAD_DIM] = o.astype(o_ref.dtype)
            lse_tile = jnp.where(lane == h, m + jnp.log(den), lse_tile)
        lse_ref[rows, :] = lse_tile


def _dil_attention(q, k, v):
    n_heads, b, r, length, _ = q.shape
    tu = min(DIL_Q_ROWS, length)
    sub = DIL_SUB
    gw = n_heads * HEAD_DIM
    cur = pl.BlockSpec((n_heads, None, None, tu, HEAD_DIM), lambda bi, c, iu: (0, bi, c, iu, 0))
    prev = pl.BlockSpec((n_heads, None, None, sub, HEAD_DIM),
                        lambda bi, c, iu: (0, bi, c, jnp.maximum(iu * (tu // sub) - 1, 0), 0))
    o, lse = pl.pallas_call(
        functools.partial(_dil_attn_kernel, scale=1.0 / math.sqrt(HEAD_DIM)),
        grid=(b, r, length // tu),
        in_specs=[cur, cur, cur, prev, prev],
        out_specs=[pl.BlockSpec((None, tu, gw), lambda bi, c, iu: (bi, iu, c)),
                   pl.BlockSpec((None, tu, LANES), lambda bi, c, iu: (bi, iu, c))],
        out_shape=[jax.ShapeDtypeStruct((b, length, r * gw), BF16),
                   jax.ShapeDtypeStruct((b, length, r * LANES), F32)],
        compiler_params=pltpu.CompilerParams(
            dimension_semantics=("parallel", "parallel", "parallel"),
            vmem_limit_bytes=_vmem_limit(6 * tu * gw * 2, 4 * sub * gw * 2, 2 * tu * gw * 2, 2 * tu * LANES * 4)),
        name=f"dil_attention_r{r}",
    )(q, k, v, k, v)
    return o.reshape(b, length * r, gw), lse.reshape(b, length * r, LANES)


def _dil_outproj_ln_kernel(o0_ref, o1_ref, o2_ref, l0_ref, l1_ref, l2_ref, w_ref, x_ref, g_ref, b_ref, y_ref, *, alpha):
    lses = [l0_ref[...], l1_ref[...], l2_ref[...]]
    mx = jnp.maximum(jnp.maximum(lses[0], lses[1]), lses[2])
    es = [jnp.exp(l - mx) for l in lses]
    den = es[0] + es[1] + es[2]
    pieces = []
    for o_ref, e in zip((o0_ref, o1_ref, o2_ref), es):
        share = e / den
        for h in range(DIL_HEADS_PER_GROUP):
            o = o_ref[:, h * HEAD_DIM:(h + 1) * HEAD_DIM].astype(F32)
            pieces.append((o * share[:, h:h + 1]).astype(BF16))
    lhs = jnp.concatenate(pieces, axis=1)
    mixed = jnp.dot(lhs, w_ref[...], preferred_element_type=F32)
    y_ref[...] = _layer_norm_rows(alpha * x_ref[...] + mixed, g_ref[...], b_ref[...])


def _dil_outproj_ln(os, lses, w, x, g, b, alpha):
    t, gw = os[0].shape
    k, d = w.shape
    tm = min(LN_ROWS, t)
    o_spec = pl.BlockSpec((tm, gw), lambda i: (i, 0))
    l_spec = pl.BlockSpec((tm, LANES), lambda i: (i, 0))
    return pl.pallas_call(
        functools.partial(_dil_outproj_ln_kernel, alpha=alpha),
        grid=(t // tm,),
        in_specs=[o_spec, o_spec, o_spec, l_spec, l_spec, l_spec,
                  pl.BlockSpec((k, d), lambda i: (0, 0)),
                  pl.BlockSpec((tm, d), lambda i: (i, 0)),
                  pl.BlockSpec((1, d), lambda i: (0, 0)),
                  pl.BlockSpec((1, d), lambda i: (0, 0))],
        out_specs=pl.BlockSpec((tm, d), lambda i: (i, 0)),
        out_shape=jax.ShapeDtypeStruct((t, d), F32),
        compiler_params=pltpu.CompilerParams(
            dimension_semantics=("parallel",),
            vmem_limit_bytes=_vmem_limit(3 * tm * k * 2, 2 * k * d * 2, 4 * tm * d * 4, 2 * tm * d * 4)),
        name="dil_outproj_ln",
    )(*os, *lses, w, x, g, b)


def kernel(x, w_qkv_sb, w_o_sb, w_qkv_dil, w_o_dil, ln_mix_g, ln_mix_b, w_gate_up, w_down, ln_ffn_g, ln_ffn_b):
    batch, seq, d_model = x.shape
    depth = ln_mix_g.shape[0]
    alpha = (2.0 * depth) ** 0.25
    t = batch * seq
    sb_heads = w_o_sb.shape[1] // HEAD_DIM
    assert all(window // dilation == DIL_SUB for window, dilation in DIL_CONFIGS)

    w_qkv_sb, w_o_sb, w_qkv_dil, w_o_dil, w_gate_up, w_down = (
        w.astype(BF16) for w in (w_qkv_sb, w_o_sb, w_qkv_dil, w_o_dil, w_gate_up, w_down))
    cos2, sin2 = _rope_tables(seq)

    xf = x.reshape(t, d_model)
    for layer in range(depth):
        j = layer // 2
        g_mix, b_mix = ln_mix_g[layer][None, :], ln_mix_b[layer][None, :]
        if layer % 2 == 0:
            qkv = _proj_heads(xf, w_qkv_sb[j])
            o = _sb_attention(qkv, batch, seq, sb_heads)
            xf = _outproj_ln(o, w_o_sb[j], xf, g_mix, b_mix, alpha)
        else:
            os, lses = [], []
            for group, (_, dilation) in enumerate(DIL_CONFIGS):
                q, k, v = _dil_proj(xf.reshape(batch, seq, d_model), w_qkv_dil[j], group, dilation, cos2, sin2)
                o, lse = _dil_attention(q, k, v)
                os.append(o.reshape(t, -1))
                lses.append(lse.reshape(t, LANES))
            xf = _dil_outproj_ln(os, lses, w_o_dil[j], xf, g_mix, b_mix, alpha)
        h = _ffn_up(xf, w_gate_up[layer])
        xf = _ffn_down_ln(h, w_down[layer], xf, ln_ffn_g[layer][None, :], ln_ffn_b[layer][None, :], alpha)
    return xf.reshape(batch, seq, d_model)
```

```python
import functools
import math

import jax
import jax.numpy as jnp
from jax import lax
from jax.experimental import pallas as pl
from jax.experimental.pallas import tpu as pltpu

HEAD_DIM = 128
DIL_CONFIGS = ((128, 1), (512, 4), (2048, 16))
DIL_HEADS_PER_GROUP = 6
ROPE_THETA = 10000.0
LN_EPS = 1e-5
LOG2_E = math.log2(math.e)

V7X_VMEM_BYTES = 64 * 1024 * 1024
V7X_MXU_DIM = 256
LANES = 128

PROJ_ROWS = 1024
PROJ_COLS = 1024
LN_ROWS = 512
FFN_COLS = 512
FFN_DOWN_ROWS = 256
SB_Q_ROWS = 512
SB_KEY_BLOCK = V7X_MXU_DIM
SB_HEADS_PER_STEP = 2
DIL_PROJ_ROWS = 512
DIL_Q_ROWS = 256
DIL_SUB = 128

BF16 = jnp.bfloat16
F32 = jnp.float32


def _vmem_limit(*nbytes):
    return int(min(V7X_VMEM_BYTES * 7 // 8, max(sum(nbytes) * 3 // 2, V7X_VMEM_BYTES // 4)))


def _layer_norm_rows(y, g, b):
    mu = jnp.mean(y, axis=-1, keepdims=True)
    d = y - mu
    var = jnp.mean(d * d, axis=-1, keepdims=True)
    return d * lax.rsqrt(var + LN_EPS) * g + b


def _proj_heads_kernel(x_ref, w_ref, o_ref, xb_ref):
    @pl.when(pl.program_id(1) == 0)
    def _():
        xb_ref[...] = x_ref[...].astype(BF16)

    acc = jnp.dot(xb_ref[...], w_ref[...], preferred_element_type=F32)
    for h in range(o_ref.shape[0]):
        o_ref[h] = acc[:, h * HEAD_DIM:(h + 1) * HEAD_DIM].astype(o_ref.dtype)


def _proj_heads(x, w, layer):
    t, k = x.shape
    n = w.shape[2]
    tm, tn = min(PROJ_ROWS, t), min(PROJ_COLS, n)
    hpt = tn // HEAD_DIM
    return pl.pallas_call(
        _proj_heads_kernel,
        grid=(t // tm, n // tn),
        in_specs=[pl.BlockSpec((tm, k), lambda i, j: (i, 0)),
                  pl.BlockSpec((None, k, tn), lambda i, j: (layer, 0, j))],
        out_specs=pl.BlockSpec((hpt, tm, HEAD_DIM), lambda i, j: (j, i, 0)),
        out_shape=jax.ShapeDtypeStruct((n // HEAD_DIM, t, HEAD_DIM), BF16),
        scratch_shapes=[pltpu.VMEM((tm, k), BF16)],
        compiler_params=pltpu.CompilerParams(
            dimension_semantics=("parallel", "arbitrary"),
            vmem_limit_bytes=_vmem_limit(2 * tm * k * 4, tm * k * 2, 2 * k * tn * 2, 2 * tm * tn * 2, tm * tn * 4)),
        name="sb_qkv_proj",
    )(x, w)


def _sb_attn_kernel(q_ref, k_ref, v_ref, tri_ref, o_ref, acc_ref, run_ref, *, scale):
    n_heads, tq, _ = q_ref.shape
    tk = SB_KEY_BLOCK
    diag_blocks = tq // tk
    i = pl.program_id(2)
    acc_ref[...] = jnp.zeros_like(acc_ref)
    run_ref[...] = jnp.zeros_like(run_ref)

    def visit(k0, row0, n_blocks, masked):
        heads = range(n_heads)
        zs, his, los = [], [], []
        for h in heads:
            q = q_ref[h, row0:, :]
            kblk = k_ref[h, pl.ds(k0, n_blocks * tk), :]
            zs.append(lax.dot_general(q, kblk, (((1,), (1,)), ((), ())), preferred_element_type=F32) * scale)
        if masked:
            qpos = i * tq + row0 + lax.broadcasted_iota(jnp.int32, zs[0].shape, 0)
            kpos = k0 + lax.broadcasted_iota(jnp.int32, zs[0].shape, 1)
            causal = kpos < qpos
        for z in zs:
            cost = jnp.maximum(z, 0.0) + jnp.log(1.0 + jnp.exp2(jnp.abs(z) * (-LOG2_E)))
            if masked:
                cost = jnp.where(causal, cost, 0.0)
            hi = cost.astype(BF16)
            his.append(hi)
            los.append((cost - hi.astype(F32)).astype(BF16))
        runs = [run_ref[h, row0:, :] for h in heads]
        pieces = [[None] * n_blocks for _ in heads]
        for s in reversed(range(n_blocks)):
            cols = slice(s * tk, (s + 1) * tk)
            for h in heads:
                incl = jnp.dot(jnp.concatenate([his[h][:, cols], los[h][:, cols]], axis=1), tri_ref[...],
                               preferred_element_type=F32)
                pieces[h][s] = jnp.exp(zs[h][:, cols] - incl - runs[h])
                runs[h] = runs[h] + incl[:, 0:1]
        for h in heads:
            w = pieces[h][0] if n_blocks == 1 else jnp.concatenate(pieces[h], axis=1)
            if masked:
                w = jnp.where(causal, w, 0.0)
            vblk = v_ref[h, pl.ds(k0, n_blocks * tk), :]
            acc_ref[h, row0:, :] += jnp.dot(w.astype(BF16), vblk, preferred_element_type=F32)
            run_ref[h, row0:, :] = runs[h]

    diag0 = pl.multiple_of(i * tq, tq)
    for d in reversed(range(diag_blocks)):
        visit(diag0 + d * tk, d * tk, 1, True)

    def body(n, carry):
        visit(pl.multiple_of((i - 1 - n) * tq, tq), 0, diag_blocks, False)
        return carry

    lax.fori_loop(0, i, body, 0)
    o_ref[...] = acc_ref[...].astype(o_ref.dtype)


def _sb_attention(qkv, batch, seq, n_heads):
    tq, tk = min(SB_Q_ROWS, seq), SB_KEY_BLOCK
    nq = seq // tq
    hp = SB_HEADS_PER_STEP
    ng = n_heads // hp
    tri = (lax.broadcasted_iota(jnp.int32, (tk, tk), 0) >= lax.broadcasted_iota(jnp.int32, (tk, tk), 1))
    tri2 = jnp.concatenate([tri, tri], axis=0).astype(BF16)
    return pl.pallas_call(
        functools.partial(_sb_attn_kernel, scale=1.0 / math.sqrt(HEAD_DIM)),
        grid=(ng, batch, nq),
        in_specs=[pl.BlockSpec((hp, tq, HEAD_DIM), lambda g, b, i: (g, b * nq + i, 0)),
                  pl.BlockSpec((hp, seq, HEAD_DIM), lambda g, b, i: (ng + g, b, 0)),
                  pl.BlockSpec((hp, seq, HEAD_DIM), lambda g, b, i: (2 * ng + g, b, 0)),
                  pl.BlockSpec((2 * tk, tk), lambda g, b, i: (0, 0))],
        out_specs=pl.BlockSpec((hp, tq, HEAD_DIM), lambda g, b, i: (g, b * nq + i, 0)),
        out_shape=jax.ShapeDtypeStruct((n_heads, batch * seq, HEAD_DIM), BF16),
        scratch_shapes=[pltpu.VMEM((hp, tq, HEAD_DIM), F32), pltpu.VMEM((hp, tq, 1), F32)],
        compiler_params=pltpu.CompilerParams(
            dimension_semantics=("parallel", "parallel", "arbitrary"),
            vmem_limit_bytes=_vmem_limit(4 * hp * seq * HEAD_DIM * 2, 8 * hp * tq * tq * 4)),
        name="sb_attention",
    )(qkv, qkv, qkv, tri2)


def _outproj_ln_kernel(o_ref, w_ref, x_ref, g_ref, b_ref, y_ref, *, alpha):
    lhs = jnp.concatenate([o_ref[h] for h in range(o_ref.shape[0])], axis=1)
    mixed = jnp.dot(lhs, w_ref[...], preferred_element_type=F32)
    y_ref[...] = _layer_norm_rows(alpha * x_ref[...] + mixed, g_ref[...], b_ref[...])


def _outproj_ln(o, w, layer, x, g, b, alpha):
    n_heads, t, _ = o.shape
    _, k, d = w.shape
    tm = min(LN_ROWS, t)
    return pl.pallas_call(
        functools.partial(_outproj_ln_kernel, alpha=alpha),
        grid=(t // tm,),
        in_specs=[pl.BlockSpec((n_heads, tm, HEAD_DIM), lambda i: (0, i, 0)),
                  pl.BlockSpec((None, k, d), lambda i: (layer, 0, 0), pipeline_mode=pl.Buffered(1)),
                  pl.BlockSpec((tm, d), lambda i: (i, 0)),
                  pl.BlockSpec((1, d), lambda i: (0, 0)),
                  pl.BlockSpec((1, d), lambda i: (0, 0))],
        out_specs=pl.BlockSpec((tm, d), lambda i: (i, 0)),
        out_shape=jax.ShapeDtypeStruct((t, d), F32),
        compiler_params=pltpu.CompilerParams(
            dimension_semantics=("parallel",),
            vmem_limit_bytes=_vmem_limit(2 * tm * k * 2, k * d * 2, 4 * tm * d * 4, 2 * tm * d * 4)),
        name="sb_outproj_ln",
    )(o, w, x, g, b)


def _ffn_up_kernel(x_ref, wg_ref, wu_ref, h_ref, xb_ref):
    @pl.when(pl.program_id(1) == 0)
    def _():
        xb_ref[...] = x_ref[...].astype(BF16)

    xb = xb_ref[...]
    gate = jnp.dot(xb, wg_ref[...], preferred_element_type=F32)
    up = jnp.dot(xb, wu_ref[...], preferred_element_type=F32)
    h_ref[...] = (jax.nn.silu(gate) * up).astype(h_ref.dtype)


def _ffn_up(x, w_gate_up, layer):
    t, d = x.shape
    f = w_gate_up.shape[2] // 2
    tm, tn = min(PROJ_ROWS, t), min(FFN_COLS, f)
    nj = f // tn
    return pl.pallas_call(
        _ffn_up_kernel,
        grid=(t // tm, nj),
        in_specs=[pl.BlockSpec((tm, d), lambda i, j: (i, 0)),
                  pl.BlockSpec((None, d, tn), lambda i, j: (layer, 0, j)),
                  pl.BlockSpec((None, d, tn), lambda i, j: (layer, 0, nj + j))],
        out_specs=pl.BlockSpec((tm, tn), lambda i, j: (i, j)),
        out_shape=jax.ShapeDtypeStruct((t, f), BF16),
        scratch_shapes=[pltpu.VMEM((tm, d), BF16)],
        compiler_params=pltpu.CompilerParams(
            dimension_semantics=("parallel", "arbitrary"),
            vmem_limit_bytes=_vmem_limit(2 * tm * d * 4, tm * d * 2, 4 * d * tn * 2, 2 * tm * tn * 2, 3 * tm * tn * 4)),
        name="ffn_up",
    )(x, w_gate_up, w_gate_up)


def _ffn_down_ln_kernel(h_ref, w_ref, x_ref, g_ref, b_ref, y_ref, *, alpha):
    down = jnp.dot(h_ref[...], w_ref[...], preferred_element_type=F32)
    y_ref[...] = _layer_norm_rows(alpha * x_ref[...] + down, g_ref[...], b_ref[...])


def _ffn_down_ln(h, w, layer, x, g, b, alpha):
    t, f = h.shape
    d = w.shape[2]
    tm = min(FFN_DOWN_ROWS, t)
    return pl.pallas_call(
        functools.partial(_ffn_down_ln_kernel, alpha=alpha),
        grid=(t // tm,),
        in_specs=[pl.BlockSpec((tm, f), lambda i: (i, 0)),
                  pl.BlockSpec((None, f, d), lambda i: (layer, 0, 0), pipeline_mode=pl.Buffered(1)),
                  pl.BlockSpec((tm, d), lambda i: (i, 0)),
                  pl.BlockSpec((1, d), lambda i: (0, 0)),
                  pl.BlockSpec((1, d), lambda i: (0, 0))],
        out_specs=pl.BlockSpec((tm, d), lambda i: (i, 0)),
        out_shape=jax.ShapeDtypeStruct((t, d), F32),
        compiler_params=pltpu.CompilerParams(
            dimension_semantics=("parallel",),
            vmem_limit_bytes=_vmem_limit(f * d * 2, 2 * tm * f * 2, 4 * tm * d * 4, 2 * tm * d * 4)),
        name="ffn_down_ln",
    )(h, w, x, g, b)


def _rope_tables(seq):
    pos = jnp.arange(seq, dtype=F32)
    inv_freq = ROPE_THETA ** (-jnp.arange(0, HEAD_DIM, 2, dtype=F32) / HEAD_DIM)
    ang = pos[:, None] * inv_freq[None, :]
    cos, sin = jnp.cos(ang), jnp.sin(ang)
    return jnp.concatenate([cos, cos], axis=1), jnp.concatenate([-sin, sin], axis=1)


def _dil_proj_kernel(x_ref, wq_ref, wk_ref, wv_ref, cos_ref, sin_ref, q_ref, k_ref, v_ref, xp_ref, *planes):
    _, r, tu, _ = q_ref.shape
    if r == 1:
        xp_ref[...] = x_ref[...].astype(BF16)
    else:
        planes_ref, = planes
        for j in range(planes_ref.shape[0]):
            planes_ref[j] = x_ref[:, j * LANES:(j + 1) * LANES]
        for c in range(r):
            for j in range(planes_ref.shape[0]):
                rows = planes_ref[j, pl.ds(c, tu, stride=r), :]
                xp_ref[c * tu:(c + 1) * tu, j * LANES:(j + 1) * LANES] = rows.astype(BF16)
    xb = xp_ref[...]
    cos, sin = cos_ref[...], sin_ref[...]
    for w_ref, out_ref, rope in ((wq_ref, q_ref, True), (wk_ref, k_ref, True), (wv_ref, v_ref, False)):
        acc = jnp.dot(xb, w_ref[...], preferred_element_type=F32)
        for h in range(out_ref.shape[0]):
            a = acc[:, h * HEAD_DIM:(h + 1) * HEAD_DIM]
            if rope:
                a = a * cos + pltpu.roll(a, HEAD_DIM // 2, 1) * sin
            a = a.astype(out_ref.dtype)
            for c in range(r):
                out_ref[h, c] = a[c * tu:(c + 1) * tu, :]


def _dil_proj(x, w, layer, group, dilation, cos2, sin2, batch):
    t, d = x.shape
    s = t // batch
    r = dilation
    length = s // r
    tm = min(DIL_PROJ_ROWS, s)
    tu = tm // r
    nt = s // tm
    planes = [] if r == 1 else [pltpu.VMEM((d // LANES, tm, LANES), F32)]
    n_groups = len(DIL_CONFIGS)
    gw = DIL_HEADS_PER_GROUP * HEAD_DIM
    cos_p = cos2.reshape(nt, tu, r, HEAD_DIM).transpose(0, 2, 1, 3).reshape(s, HEAD_DIM)
    sin_p = sin2.reshape(nt, tu, r, HEAD_DIM).transpose(0, 2, 1, 3).reshape(s, HEAD_DIM)
    w_spec = lambda kind: pl.BlockSpec((None, d, gw), lambda i: (layer, 0, kind * n_groups + group),
                                       pipeline_mode=pl.Buffered(1))
    tab_spec = pl.BlockSpec((tm, HEAD_DIM), lambda i: (i % nt, 0))
    out_spec = pl.BlockSpec((DIL_HEADS_PER_GROUP, None, r, tu, HEAD_DIM), lambda i: (0, i // nt, 0, i % nt, 0))
    out_sds = jax.ShapeDtypeStruct((DIL_HEADS_PER_GROUP, batch, r, length, HEAD_DIM), BF16)
    return pl.pallas_call(
        _dil_proj_kernel,
        grid=(t // tm,),
        in_specs=[pl.BlockSpec((tm, d), lambda i: (i, 0)),
                  w_spec(0), w_spec(1), w_spec(2), tab_spec, tab_spec],
        out_specs=[out_spec, out_spec, out_spec],
        out_shape=[out_sds, out_sds, out_sds],
        scratch_shapes=[pltpu.VMEM((tm, d), BF16)] + planes,
        compiler_params=pltpu.CompilerParams(
            dimension_semantics=("parallel",),
            vmem_limit_bytes=_vmem_limit(3 * tm * d * 4, tm * d * 2, 3 * d * gw * 2, 6 * tm * gw * 2, 2 * tm * gw * 4)),
        name=f"dil_qkv_proj_r{r}",
    )(x, w, w, w, cos_p, sin_p)


def _dil_attn_kernel(q_ref, k_ref, v_ref, kp_ref, vp_ref, o_ref, lse_ref, *, scale):
    n_heads, tu, _ = q_ref.shape
    sub = DIL_SUB
    iu = pl.program_id(2)
    qi = lax.broadcasted_iota(jnp.int32, (sub, 2 * sub), 0)
    kj = lax.broadcasted_iota(jnp.int32, (sub, 2 * sub), 1)
    in_window = (kj >= qi) & (kj <= qi + sub)
    lane = lax.broadcasted_iota(jnp.int32, (sub, LANES), 1)
    for s in range(tu // sub):
        rows = slice(s * sub, (s + 1) * sub)
        if s == 0:
            valid = in_window & (kj >= jnp.where(iu > 0, 0, sub))
        else:
            valid = in_window
        lse_tile = jnp.zeros((sub, LANES), F32)
        for h in range(n_heads):
            q = q_ref[h, rows, :]
            if s == 0:
                k_prev, v_prev = kp_ref[h], vp_ref[h]
            else:
                k_prev, v_prev = k_ref[h, (s - 1) * sub:s * sub, :], v_ref[h, (s - 1) * sub:s * sub, :]
            keys = jnp.concatenate([k_prev, k_ref[h, rows, :]], axis=0)
            vals = jnp.concatenate([v_prev, v_ref[h, rows, :]], axis=0)
            sc = lax.dot_general(q, keys, (((1,), (1,)), ((), ())), preferred_element_type=F32) * scale
            sc = jnp.where(valid, sc, -jnp.inf)
            m = jnp.max(sc, axis=1, keepdims=True)
            p = jnp.exp(sc - m)
            den = jnp.sum(p, axis=1, keepdims=True)
            o = jnp.dot(p.astype(BF16), vals, preferred_element_type=F32) / den
            o_ref[rows, h * HEAD_DIM:(h + 1) * HEAD_DIM] = o.astype(o_ref.dtype)
            lse_tile = jnp.where(lane == h, m + jnp.log(den), lse_tile)
        lse_ref[rows, :] = lse_tile


def _dil_attention(q, k, v):
    n_heads, b, r, length, _ = q.shape
    tu = min(DIL_Q_ROWS, length)
    sub = DIL_SUB
    gw = n_heads * HEAD_DIM
    cur = pl.BlockSpec((n_heads, None, None, tu, HEAD_DIM), lambda bi, c, iu: (0, bi, c, iu, 0))
    prev = pl.BlockSpec((n_heads, None, None, sub, HEAD_DIM),
                        lambda bi, c, iu: (0, bi, c, jnp.maximum(iu * (tu // sub) - 1, 0), 0))
    o, lse = pl.pallas_call(
        functools.partial(_dil_attn_kernel, scale=1.0 / math.sqrt(HEAD_DIM)),
        grid=(b, r, length // tu),
        in_specs=[cur, cur, cur, prev, prev],
        out_specs=[pl.BlockSpec((None, tu, gw), lambda bi, c, iu: (bi, iu, c)),
                   pl.BlockSpec((None, tu, LANES), lambda bi, c, iu: (bi, iu, c))],
        out_shape=[jax.ShapeDtypeStruct((b, length, r * gw), BF16),
                   jax.ShapeDtypeStruct((b, length, r * LANES), F32)],
        compiler_params=pltpu.CompilerParams(
            dimension_semantics=("parallel", "parallel", "parallel"),
            vmem_limit_bytes=_vmem_limit(6 * tu * gw * 2, 4 * sub * gw * 2, 2 * tu * gw * 2, 2 * tu * LANES * 4)),
        name=f"dil_attention_r{r}",
    )(q, k, v, k, v)
    return o.reshape(b, length * r, gw), lse.reshape(b, length * r, LANES)


def _dil_outproj_ln_kernel(o0_ref, o1_ref, o2_ref, l0_ref, l1_ref, l2_ref, w_ref, x_ref, g_ref, b_ref, y_ref, *, alpha):
    lses = [l0_ref[...], l1_ref[...], l2_ref[...]]
    mx = jnp.maximum(jnp.maximum(lses[0], lses[1]), lses[2])
    es = [jnp.exp(l - mx) for l in lses]
    den = es[0] + es[1] + es[2]
    pieces = []
    for o_ref, e in zip((o0_ref, o1_ref, o2_ref), es):
        share = e / den
        for h in range(DIL_HEADS_PER_GROUP):
            o = o_ref[:, h * HEAD_DIM:(h + 1) * HEAD_DIM].astype(F32)
            pieces.append((o * share[:, h:h + 1]).astype(BF16))
    lhs = jnp.concatenate(pieces, axis=1)
    mixed = jnp.dot(lhs, w_ref[...], preferred_element_type=F32)
    y_ref[...] = _layer_norm_rows(alpha * x_ref[...] + mixed, g_ref[...], b_ref[...])


def _dil_outproj_ln(os, lses, w, layer, x, g, b, alpha):
    t, gw = os[0].shape
    _, k, d = w.shape
    tm = min(LN_ROWS, t)
    o_spec = pl.BlockSpec((tm, gw), lambda i: (i, 0))
    l_spec = pl.BlockSpec((tm, LANES), lambda i: (i, 0))
    return pl.pallas_call(
        functools.partial(_dil_outproj_ln_kernel, alpha=alpha),
        grid=(t // tm,),
        in_specs=[o_spec, o_spec, o_spec, l_spec, l_spec, l_spec,
                  pl.BlockSpec((None, k, d), lambda i: (layer, 0, 0), pipeline_mode=pl.Buffered(1)),
                  pl.BlockSpec((tm, d), lambda i: (i, 0)),
                  pl.BlockSpec((1, d), lambda i: (0, 0)),
                  pl.BlockSpec((1, d), lambda i: (0, 0))],
        out_specs=pl.BlockSpec((tm, d), lambda i: (i, 0)),
        out_shape=jax.ShapeDtypeStruct((t, d), F32),
        compiler_params=pltpu.CompilerParams(
            dimension_semantics=("parallel",),
            vmem_limit_bytes=_vmem_limit(3 * tm * k * 2, k * d * 2, 4 * tm * d * 4, 2 * tm * d * 4)),
        name="dil_outproj_ln",
    )(*os, *lses, w, x, g, b)


def kernel(x, w_qkv_sb, w_o_sb, w_qkv_dil, w_o_dil, ln_mix_g, ln_mix_b, w_gate_up, w_down, ln_ffn_g, ln_ffn_b):
    batch, seq, d_model = x.shape
    depth = ln_mix_g.shape[0]
    alpha = (2.0 * depth) ** 0.25
    t = batch * seq
    sb_heads = w_o_sb.shape[1] // HEAD_DIM
    assert all(window // dilation == DIL_SUB for window, dilation in DIL_CONFIGS)

    w_qkv_sb, w_o_sb, w_qkv_dil, w_o_dil, w_gate_up, w_down = (
        w.astype(BF16) for w in (w_qkv_sb, w_o_sb, w_qkv_dil, w_o_dil, w_gate_up, w_down))
    cos2, sin2 = _rope_tables(seq)

    xf = x.reshape(t, d_model)
    for layer in range(depth):
        j = layer // 2
        g_mix, b_mix = ln_mix_g[layer][None, :], ln_mix_b[layer][None, :]
        if layer % 2 == 0:
            qkv = _proj_heads(xf, w_qkv_sb, j)
            o = _sb_attention(qkv, batch, seq, sb_heads)
            xf = _outproj_ln(o, w_o_sb, j, xf, g_mix, b_mix, alpha)
        else:
            os, lses = [], []
            for group, (_, dilation) in enumerate(DIL_CONFIGS):
                q, k, v = _dil_proj(xf, w_qkv_dil, j, group, dilation, cos2, sin2, batch)
                o, lse = _dil_attention(q, k, v)
                os.append(o.reshape(t, -1))
                lses.append(lse.reshape(t, LANES))
            xf = _dil_outproj_ln(os, lses, w_o_dil, j, xf, g_mix, b_mix, alpha)
        h = _ffn_up(xf, w_gate_up, layer)
        xf = _ffn_down_ln(h, w_down, layer, xf, ln_ffn_g[layer][None, :], ln_ffn_b[layer][None, :], alpha)
    return xf.reshape(batch, seq, d_model)
```

```python
import functools
import math

import jax
import jax.numpy as jnp
from jax import lax
from jax.experimental import pallas as pl
from jax.experimental.pallas import tpu as pltpu

HEAD_DIM = 128
DIL_CONFIGS = ((128, 1), (512, 4), (2048, 16))
DIL_HEADS_PER_GROUP = 6
ROPE_THETA = 10000.0
LN_EPS = 1e-5
LOG2_E = math.log2(math.e)

V7X_VMEM_BYTES = 64 * 1024 * 1024
V7X_MXU_DIM = 256
LANES = 128

PROJ_ROWS = 1024
PROJ_COLS = 1024
LN_ROWS = 512
FFN_COLS = 512
FFN_DOWN_ROWS = 256
SB_Q_ROWS = 512
SB_KEY_BLOCK = V7X_MXU_DIM
SB_HEADS_PER_STEP = 4
DIL_PROJ_ROWS = 512
DIL_Q_ROWS = 256
DIL_SUB = 128

BF16 = jnp.bfloat16
F32 = jnp.float32


def _vmem_limit(*nbytes):
    return int(min(V7X_VMEM_BYTES * 7 // 8, max(sum(nbytes) * 3 // 2, V7X_VMEM_BYTES // 4)))


def _layer_norm_rows(y, g, b):
    mu = jnp.mean(y, axis=-1, keepdims=True)
    d = y - mu
    var = jnp.mean(d * d, axis=-1, keepdims=True)
    return d * lax.rsqrt(var + LN_EPS) * g + b


def _proj_heads_kernel(x_ref, w_ref, o_ref, xb_ref, *, q_tiles, q_scale):
    j = pl.program_id(1)

    @pl.when(j == 0)
    def _():
        xb_ref[...] = x_ref[...].astype(BF16)

    acc = jnp.dot(xb_ref[...], w_ref[...], preferred_element_type=F32)
    acc = acc * jnp.where(j < q_tiles, q_scale, 1.0)
    for h in range(o_ref.shape[0]):
        o_ref[h] = acc[:, h * HEAD_DIM:(h + 1) * HEAD_DIM].astype(o_ref.dtype)


def _proj_heads(x, w, layer, q_scale):
    t, k = x.shape
    n = w.shape[2]
    tm, tn = min(PROJ_ROWS, t), min(PROJ_COLS, n // 3)
    hpt = tn // HEAD_DIM
    return pl.pallas_call(
        functools.partial(_proj_heads_kernel, q_tiles=n // 3 // tn, q_scale=q_scale),
        grid=(t // tm, n // tn),
        in_specs=[pl.BlockSpec((tm, k), lambda i, j: (i, 0)),
                  pl.BlockSpec((None, k, tn), lambda i, j: (layer, 0, j))],
        out_specs=pl.BlockSpec((hpt, tm, HEAD_DIM), lambda i, j: (j, i, 0)),
        out_shape=jax.ShapeDtypeStruct((n // HEAD_DIM, t, HEAD_DIM), BF16),
        scratch_shapes=[pltpu.VMEM((tm, k), BF16)],
        compiler_params=pltpu.CompilerParams(
            dimension_semantics=("parallel", "arbitrary"),
            vmem_limit_bytes=_vmem_limit(2 * tm * k * 4, tm * k * 2, 2 * k * tn * 2, 2 * tm * tn * 2, tm * tn * 4)),
        name="sb_qkv_proj",
    )(x, w)


def _sb_attn_kernel(q_ref, k_ref, v_ref, tri_ref, o_ref, acc_ref, run_ref):
    n_heads, tq, _ = q_ref.shape
    tk = SB_KEY_BLOCK
    diag_blocks = tq // tk
    i = pl.program_id(2)
    acc_ref[...] = jnp.zeros_like(acc_ref)
    run_ref[...] = jnp.zeros_like(run_ref)

    def visit(k0, row0, n_blocks, masked):
        heads = range(n_heads)
        zs, costs = [], []
        for h in heads:
            q = q_ref[h, row0:, :]
            kblk = k_ref[h, pl.ds(k0, n_blocks * tk), :]
            zs.append(lax.dot_general(q, kblk, (((1,), (1,)), ((), ())), preferred_element_type=F32))
        if masked:
            qpos = i * tq + row0 + lax.broadcasted_iota(jnp.int32, zs[0].shape, 0)
            kpos = k0 + lax.broadcasted_iota(jnp.int32, zs[0].shape, 1)
            causal = kpos < qpos
        for z in zs:
            neg_abs = pltpu.bitcast(pltpu.bitcast(z, jnp.uint32) | jnp.uint32(0x80000000), F32)
            cost = jnp.maximum(z, 0.0) + jnp.log(1.0 + jnp.exp2(neg_abs)) * LOG2_E
            if masked:
                cost = jnp.where(causal, cost, 0.0)
            costs.append(cost.astype(BF16))
        runs = [run_ref[h, row0:, :] for h in heads]
        pieces = [[None] * n_blocks for _ in heads]
        for s in reversed(range(n_blocks)):
            cols = slice(s * tk, (s + 1) * tk)
            for h in heads:
                incl = jnp.dot(costs[h][:, cols], tri_ref[...], preferred_element_type=F32)
                pieces[h][s] = jnp.exp2(zs[h][:, cols] - incl - runs[h])
                runs[h] = runs[h] + incl[:, 0:1]
        for h in heads:
            w = pieces[h][0] if n_blocks == 1 else jnp.concatenate(pieces[h], axis=1)
            if masked:
                w = jnp.where(causal, w, 0.0)
            vblk = v_ref[h, pl.ds(k0, n_blocks * tk), :]
            acc_ref[h, row0:, :] += jnp.dot(w.astype(BF16), vblk, preferred_element_type=F32)
            run_ref[h, row0:, :] = runs[h]

    diag0 = pl.multiple_of(i * tq, tq)
    for d in reversed(range(diag_blocks)):
        visit(diag0 + d * tk, d * tk, 1, True)

    def body(n, carry):
        visit(pl.multiple_of((i - 1 - n) * tq, tq), 0, diag_blocks, False)
        return carry

    lax.fori_loop(0, i, body, 0)
    o_ref[...] = acc_ref[...].astype(o_ref.dtype)


def _sb_attention(qkv, batch, seq, n_heads):
    tq, tk = min(SB_Q_ROWS, seq), SB_KEY_BLOCK
    nq = seq // tq
    hp = SB_HEADS_PER_STEP
    ng = n_heads // hp
    tri = (lax.broadcasted_iota(jnp.int32, (tk, tk), 0) >= lax.broadcasted_iota(jnp.int32, (tk, tk), 1)).astype(BF16)
    return pl.pallas_call(
        _sb_attn_kernel,
        grid=(ng, batch, nq),
        in_specs=[pl.BlockSpec((hp, tq, HEAD_DIM), lambda g, b, i: (g, b * nq + i, 0)),
                  pl.BlockSpec((hp, seq, HEAD_DIM), lambda g, b, i: (ng + g, b, 0)),
                  pl.BlockSpec((hp, seq, HEAD_DIM), lambda g, b, i: (2 * ng + g, b, 0)),
                  pl.BlockSpec((tk, tk), lambda g, b, i: (0, 0))],
        out_specs=pl.BlockSpec((hp, tq, HEAD_DIM), lambda g, b, i: (g, b * nq + i, 0)),
        out_shape=jax.ShapeDtypeStruct((n_heads, batch * seq, HEAD_DIM), BF16),
        scratch_shapes=[pltpu.VMEM((hp, tq, HEAD_DIM), F32), pltpu.VMEM((hp, tq, 1), F32)],
        compiler_params=pltpu.CompilerParams(
            dimension_semantics=("parallel", "parallel", "arbitrary"),
            vmem_limit_bytes=_vmem_limit(4 * hp * seq * HEAD_DIM * 2, 8 * hp * tq * tq * 4)),
        name="sb_attention",
    )(qkv, qkv, qkv, tri)


def _outproj_ln_kernel(o_ref, w_ref, x_ref, g_ref, b_ref, y_ref, *, alpha):
    lhs = jnp.concatenate([o_ref[h] for h in range(o_ref.shape[0])], axis=1)
    mixed = jnp.dot(lhs, w_ref[...], preferred_element_type=F32)
    y_ref[...] = _layer_norm_rows(alpha * x_ref[...] + mixed, g_ref[...], b_ref[...])


def _outproj_ln(o, w, layer, x, g, b, alpha):
    n_heads, t, _ = o.shape
    _, k, d = w.shape
    tm = min(LN_ROWS, t)
    return pl.pallas_call(
        functools.partial(_outproj_ln_kernel, alpha=alpha),
        grid=(t // tm,),
        in_specs=[pl.BlockSpec((n_heads, tm, HEAD_DIM), lambda i: (0, i, 0)),
                  pl.BlockSpec((None, k, d), lambda i: (layer, 0, 0), pipeline_mode=pl.Buffered(1)),
                  pl.BlockSpec((tm, d), lambda i: (i, 0)),
                  pl.BlockSpec((1, d), lambda i: (0, 0)),
                  pl.BlockSpec((1, d), lambda i: (0, 0))],
        out_specs=pl.BlockSpec((tm, d), lambda i: (i, 0)),
        out_shape=jax.ShapeDtypeStruct((t, d), F32),
        compiler_params=pltpu.CompilerParams(
            dimension_semantics=("parallel",),
            vmem_limit_bytes=_vmem_limit(2 * tm * k * 2, k * d * 2, 4 * tm * d * 4, 2 * tm * d * 4)),
        name="sb_outproj_ln",
    )(o, w, x, g, b)


def _ffn_up_kernel(x_ref, wg_ref, wu_ref, h_ref, xb_ref):
    @pl.when(pl.program_id(1) == 0)
    def _():
        xb_ref[...] = x_ref[...].astype(BF16)

    xb = xb_ref[...]
    gate = jnp.dot(xb, wg_ref[...], preferred_element_type=F32)
    up = jnp.dot(xb, wu_ref[...], preferred_element_type=F32)
    h_ref[...] = (jax.nn.silu(gate) * up).astype(h_ref.dtype)


def _ffn_up(x, w_gate_up, layer):
    t, d = x.shape
    f = w_gate_up.shape[2] // 2
    tm, tn = min(PROJ_ROWS, t), min(FFN_COLS, f)
    nj = f // tn
    return pl.pallas_call(
        _ffn_up_kernel,
        grid=(t // tm, nj),
        in_specs=[pl.BlockSpec((tm, d), lambda i, j: (i, 0)),
                  pl.BlockSpec((None, d, tn), lambda i, j: (layer, 0, j)),
                  pl.BlockSpec((None, d, tn), lambda i, j: (layer, 0, nj + j))],
        out_specs=pl.BlockSpec((tm, tn), lambda i, j: (i, j)),
        out_shape=jax.ShapeDtypeStruct((t, f), BF16),
        scratch_shapes=[pltpu.VMEM((tm, d), BF16)],
        compiler_params=pltpu.CompilerParams(
            dimension_semantics=("parallel", "arbitrary"),
            vmem_limit_bytes=_vmem_limit(2 * tm * d * 4, tm * d * 2, 4 * d * tn * 2, 2 * tm * tn * 2, 3 * tm * tn * 4)),
        name="ffn_up",
    )(x, w_gate_up, w_gate_up)


def _ffn_down_ln_kernel(h_ref, w_ref, x_ref, g_ref, b_ref, y_ref, *, alpha):
    down = jnp.dot(h_ref[...], w_ref[...], preferred_element_type=F32)
    y_ref[...] = _layer_norm_rows(alpha * x_ref[...] + down, g_ref[...], b_ref[...])


def _ffn_down_ln(h, w, layer, x, g, b, alpha):
    t, f = h.shape
    d = w.shape[2]
    tm = min(FFN_DOWN_ROWS, t)
    return pl.pallas_call(
        functools.partial(_ffn_down_ln_kernel, alpha=alpha),
        grid=(t // tm,),
        in_specs=[pl.BlockSpec((tm, f), lambda i: (i, 0)),
                  pl.BlockSpec((None, f, d), lambda i: (layer, 0, 0), pipeline_mode=pl.Buffered(1)),
                  pl.BlockSpec((tm, d), lambda i: (i, 0)),
                  pl.BlockSpec((1, d), lambda i: (0, 0)),
                  pl.BlockSpec((1, d), lambda i: (0, 0))],
        out_specs=pl.BlockSpec((tm, d), lambda i: (i, 0)),
        out_shape=jax.ShapeDtypeStruct((t, d), F32),
        compiler_params=pltpu.CompilerParams(
            dimension_semantics=("parallel",),
            vmem_limit_bytes=_vmem_limit(f * d * 2, 2 * tm * f * 2, 4 * tm * d * 4, 2 * tm * d * 4)),
        name="ffn_down_ln",
    )(h, w, x, g, b)


def _rope_tables(seq):
    pos = jnp.arange(seq, dtype=F32)
    inv_freq = ROPE_THETA ** (-jnp.arange(0, HEAD_DIM, 2, dtype=F32) / HEAD_DIM)
    ang = pos[:, None] * inv_freq[None, :]
    cos, sin = jnp.cos(ang), jnp.sin(ang)
    return jnp.concatenate([cos, cos], axis=1), jnp.concatenate([-sin, sin], axis=1)


def _dil_proj_kernel(x_ref, wq_ref, wk_ref, wv_ref, cos_ref, sin_ref, q_ref, k_ref, v_ref, xp_ref, *planes):
    _, r, tu, _ = q_ref.shape
    if r == 1:
        xp_ref[...] = x_ref[...].astype(BF16)
    else:
        planes_ref, = planes
        for j in range(planes_ref.shape[0]):
            planes_ref[j] = x_ref[:, j * LANES:(j + 1) * LANES]
        for c in range(r):
            for j in range(planes_ref.shape[0]):
                rows = planes_ref[j, pl.ds(c, tu, stride=r), :]
                xp_ref[c * tu:(c + 1) * tu, j * LANES:(j + 1) * LANES] = rows.astype(BF16)
    xb = xp_ref[...]
    cos, sin = cos_ref[...], sin_ref[...]
    for w_ref, out_ref, rope in ((wq_ref, q_ref, True), (wk_ref, k_ref, True), (wv_ref, v_ref, False)):
        acc = jnp.dot(xb, w_ref[...], preferred_element_type=F32)
        for h in range(out_ref.shape[0]):
            a = acc[:, h * HEAD_DIM:(h + 1) * HEAD_DIM]
            if rope:
                a = a * cos + pltpu.roll(a, HEAD_DIM // 2, 1) * sin
            a = a.astype(out_ref.dtype)
            for c in range(r):
                out_ref[h, c] = a[c * tu:(c + 1) * tu, :]


def _dil_proj(x, w, layer, group, dilation, cos2, sin2, batch):
    t, d = x.shape
    s = t // batch
    r = dilation
    length = s // r
    tm = min(DIL_PROJ_ROWS, s)
    tu = tm // r
    nt = s // tm
    planes = [] if r == 1 else [pltpu.VMEM((d // LANES, tm, LANES), F32)]
    n_groups = len(DIL_CONFIGS)
    gw = DIL_HEADS_PER_GROUP * HEAD_DIM
    cos_p = cos2.reshape(nt, tu, r, HEAD_DIM).transpose(0, 2, 1, 3).reshape(s, HEAD_DIM)
    sin_p = sin2.reshape(nt, tu, r, HEAD_DIM).transpose(0, 2, 1, 3).reshape(s, HEAD_DIM)
    w_spec = lambda kind: pl.BlockSpec((None, d, gw), lambda i: (layer, 0, kind * n_groups + group),
                                       pipeline_mode=pl.Buffered(1))
    tab_spec = pl.BlockSpec((tm, HEAD_DIM), lambda i: (i % nt, 0))
    out_spec = pl.BlockSpec((DIL_HEADS_PER_GROUP, None, r, tu, HEAD_DIM), lambda i: (0, i // nt, 0, i % nt, 0))
    out_sds = jax.ShapeDtypeStruct((DIL_HEADS_PER_GROUP, batch, r, length, HEAD_DIM), BF16)
    return pl.pallas_call(
        _dil_proj_kernel,
        grid=(t // tm,),
        in_specs=[pl.BlockSpec((tm, d), lambda i: (i, 0)),
                  w_spec(0), w_spec(1), w_spec(2), tab_spec, tab_spec],
        out_specs=[out_spec, out_spec, out_spec],
        out_shape=[out_sds, out_sds, out_sds],
        scratch_shapes=[pltpu.VMEM((tm, d), BF16)] + planes,
        compiler_params=pltpu.CompilerParams(
            dimension_semantics=("parallel",),
            vmem_limit_bytes=_vmem_limit(3 * tm * d * 4, tm * d * 2, 3 * d * gw * 2, 6 * tm * gw * 2, 2 * tm * gw * 4)),
        name=f"dil_qkv_proj_r{r}",
    )(x, w, w, w, cos_p, sin_p)


def _dil_attn_kernel(q_ref, k_ref, v_ref, kp_ref, vp_ref, o_ref, lse_ref, *, scale):
    n_heads, tu, _ = q_ref.shape
    sub = DIL_SUB
    iu = pl.program_id(2)
    qi = lax.broadcasted_iota(jnp.int32, (sub, 2 * sub), 0)
    kj = lax.broadcasted_iota(jnp.int32, (sub, 2 * sub), 1)
    in_window = (kj >= qi) & (kj <= qi + sub)
    lane = lax.broadcasted_iota(jnp.int32, (sub, LANES), 1)
    for s in range(tu // sub):
        rows = slice(s * sub, (s + 1) * sub)
        if s == 0:
            valid = in_window & (kj >= jnp.where(iu > 0, 0, sub))
        else:
            valid = in_window
        lse_tile = jnp.zeros((sub, LANES), F32)
        for h in range(n_heads):
            q = q_ref[h, rows, :]
            if s == 0:
                k_prev, v_prev = kp_ref[h], vp_ref[h]
            else:
                k_prev, v_prev = k_ref[h, (s - 1) * sub:s * sub, :], v_ref[h, (s - 1) * sub:s * sub, :]
            keys = jnp.concatenate([k_prev, k_ref[h, rows, :]], axis=0)
            vals = jnp.concatenate([v_prev, v_ref[h, rows, :]], axis=0)
            sc = lax.dot_general(q, keys, (((1,), (1,)), ((), ())), preferred_element_type=F32) * scale
            sc = jnp.where(valid, sc, -jnp.inf)
            m = jnp.max(sc, axis=1, keepdims=True)
            p = jnp.exp(sc - m)
            den = jnp.sum(p, axis=1, keepdims=True)
            o = jnp.dot(p.astype(BF16), vals, preferred_element_type=F32) / den
            o_ref[rows, h * HEAD_DIM:(h + 1) * HEAD_DIM] = o.astype(o_ref.dtype)
            lse_tile = jnp.where(lane == h, m + jnp.log(den), lse_tile)
        lse_ref[rows, :] = lse_tile


def _dil_attention(q, k, v):
    n_heads, b, r, length, _ = q.shape
    tu = min(DIL_Q_ROWS, length)
    sub = DIL_SUB
    gw = n_heads * HEAD_DIM
    cur = pl.BlockSpec((n_heads, None, None, tu, HEAD_DIM), lambda bi, c, iu: (0, bi, c, iu, 0))
    prev = pl.BlockSpec((n_heads, None, None, sub, HEAD_DIM),
                        lambda bi, c, iu: (0, bi, c, jnp.maximum(iu * (tu // sub) - 1, 0), 0))
    o, lse = pl.pallas_call(
        functools.partial(_dil_attn_kernel, scale=1.0 / math.sqrt(HEAD_DIM)),
        grid=(b, r, length // tu),
        in_specs=[cur, cur, cur, prev, prev],
        out_specs=[pl.BlockSpec((None, tu, gw), lambda bi, c, iu: (bi, iu, c)),
                   pl.BlockSpec((None, tu, LANES), lambda bi, c, iu: (bi, iu, c))],
        out_shape=[jax.ShapeDtypeStruct((b, length, r * gw), BF16),
                   jax.ShapeDtypeStruct((b, length, r * LANES), F32)],
        compiler_params=pltpu.CompilerParams(
            dimension_semantics=("parallel", "parallel", "parallel"),
            vmem_limit_bytes=_vmem_limit(6 * tu * gw * 2, 4 * sub * gw * 2, 2 * tu * gw * 2, 2 * tu * LANES * 4)),
        name=f"dil_attention_r{r}",
    )(q, k, v, k, v)
    return o.reshape(b, length * r, gw), lse.reshape(b, length * r, LANES)


def _dil_outproj_ln_kernel(o0_ref, o1_ref, o2_ref, l0_ref, l1_ref, l2_ref, w_ref, x_ref, g_ref, b_ref, y_ref, *, alpha):
    lses = [l0_ref[...], l1_ref[...], l2_ref[...]]
    mx = jnp.maximum(jnp.maximum(lses[0], lses[1]), lses[2])
    es = [jnp.exp(l - mx) for l in lses]
    den = es[0] + es[1] + es[2]
    pieces = []
    for o_ref, e in zip((o0_ref, o1_ref, o2_ref), es):
        share = e / den
        for h in range(DIL_HEADS_PER_GROUP):
            o = o_ref[:, h * HEAD_DIM:(h + 1) * HEAD_DIM].astype(F32)
            pieces.append((o * share[:, h:h + 1]).astype(BF16))
    lhs = jnp.concatenate(pieces, axis=1)
    mixed = jnp.dot(lhs, w_ref[...], preferred_element_type=F32)
    y_ref[...] = _layer_norm_rows(alpha * x_ref[...] + mixed, g_ref[...], b_ref[...])


def _dil_outproj_ln(os, lses, w, layer, x, g, b, alpha):
    t, gw = os[0].shape
    _, k, d = w.shape
    tm = min(LN_ROWS, t)
    o_spec = pl.BlockSpec((tm, gw), lambda i: (i, 0))
    l_spec = pl.BlockSpec((tm, LANES), lambda i: (i, 0))
    return pl.pallas_call(
        functools.partial(_dil_outproj_ln_kernel, alpha=alpha),
        grid=(t // tm,),
        in_specs=[o_spec, o_spec, o_spec, l_spec, l_spec, l_spec,
                  pl.BlockSpec((None, k, d), lambda i: (layer, 0, 0), pipeline_mode=pl.Buffered(1)),
                  pl.BlockSpec((tm, d), lambda i: (i, 0)),
                  pl.BlockSpec((1, d), lambda i: (0, 0)),
                  pl.BlockSpec((1, d), lambda i: (0, 0))],
        out_specs=pl.BlockSpec((tm, d), lambda i: (i, 0)),
        out_shape=jax.ShapeDtypeStruct((t, d), F32),
        compiler_params=pltpu.CompilerParams(
            dimension_semantics=("parallel",),
            vmem_limit_bytes=_vmem_limit(3 * tm * k * 2, k * d * 2, 4 * tm * d * 4, 2 * tm * d * 4)),
        name="dil_outproj_ln",
    )(*os, *lses, w, x, g, b)


def kernel(x, w_qkv_sb, w_o_sb, w_qkv_dil, w_o_dil, ln_mix_g, ln_mix_b, w_gate_up, w_down, ln_ffn_g, ln_ffn_b):
    batch, seq, d_model = x.shape
    depth = ln_mix_g.shape[0]
    alpha = (2.0 * depth) ** 0.25
    t = batch * seq
    sb_heads = w_o_sb.shape[1] // HEAD_DIM
    assert all(window // dilation == DIL_SUB for window, dilation in DIL_CONFIGS)

    w_qkv_sb, w_o_sb, w_qkv_dil, w_o_dil, w_gate_up, w_down = (
        w.astype(BF16) for w in (w_qkv_sb, w_o_sb, w_qkv_dil, w_o_dil, w_gate_up, w_down))
    cos2, sin2 = _rope_tables(seq)

    xf = x.reshape(t, d_model)
    for layer in range(depth):
        j = layer // 2
        g_mix, b_mix = ln_mix_g[layer][None, :], ln_mix_b[layer][None, :]
        if layer % 2 == 0:
            qkv = _proj_heads(xf, w_qkv_sb, j, LOG2_E / math.sqrt(HEAD_DIM))
            o = _sb_attention(qkv, batch, seq, sb_heads)
            xf = _outproj_ln(o, w_o_sb, j, xf, g_mix, b_mix, alpha)
        else:
            os, lses = [], []
            for group, (_, dilation) in enumerate(DIL_CONFIGS):
                q, k, v = _dil_proj(xf, w_qkv_dil, j, group, dilation, cos2, sin2, batch)
                o, lse = _dil_attention(q, k, v)
                os.append(o.reshape(t, -1))
                lses.append(lse.reshape(t, LANES))
            xf = _dil_outproj_ln(os, lses, w_o_dil, j, xf, g_mix, b_mix, alpha)
        h = _ffn_up(xf, w_gate_up, layer)
        xf = _ffn_down_ln(h, w_down, layer, xf, ln_ffn_g[layer][None, :], ln_ffn_b[layer][None, :], alpha)
    return xf.reshape(batch, seq, d_model)
```

```python
import functools
import math

import jax
import jax.numpy as jnp
from jax import lax
from jax.experimental import pallas as pl
from jax.experimental.pallas import tpu as pltpu

HEAD_DIM = 128
DIL_CONFIGS = ((128, 1), (512, 4), (2048, 16))
DIL_HEADS_PER_GROUP = 6
ROPE_THETA = 10000.0
LN_EPS = 1e-5
LOG2_E = math.log2(math.e)

V7X_VMEM_BYTES = 64 * 1024 * 1024
V7X_MXU_DIM = 256
LANES = 128

PROJ_ROWS = 1024
PROJ_COLS = 1024
LN_ROWS = 512
LN_CHUNK_ROWS = 128
FFN_COLS = 512
FFN_DOWN_ROWS = 256
SB_Q_ROWS = 512
SB_KEY_BLOCK = V7X_MXU_DIM
SB_HEADS_PER_STEP = 4
DIL_PROJ_ROWS = 512
DIL_Q_ROWS = 256
DIL_SUB = 128

BF16 = jnp.bfloat16
F32 = jnp.float32


def _vmem_limit(*nbytes):
    return int(min(V7X_VMEM_BYTES * 7 // 8, max(sum(nbytes) * 3 // 2, V7X_VMEM_BYTES // 4)))


def _layer_norm_rows(y, g, b):
    mu = jnp.mean(y, axis=-1, keepdims=True)
    d = y - mu
    var = jnp.mean(d * d, axis=-1, keepdims=True)
    return d * lax.rsqrt(var + LN_EPS) * g + b


def _proj_heads_kernel(x_ref, w_ref, o_ref, xb_ref, *, q_tiles, q_scale):
    j = pl.program_id(1)

    @pl.when(j == 0)
    def _():
        xb_ref[...] = x_ref[...].astype(BF16)

    acc = jnp.dot(xb_ref[...], w_ref[...], preferred_element_type=F32)
    acc = acc * jnp.where(j < q_tiles, q_scale, 1.0)
    for h in range(o_ref.shape[0]):
        o_ref[h] = acc[:, h * HEAD_DIM:(h + 1) * HEAD_DIM].astype(o_ref.dtype)


def _proj_heads(x, w, layer, q_scale):
    t, k = x.shape
    n = w.shape[2]
    tm, tn = min(PROJ_ROWS, t), min(PROJ_COLS, n // 3)
    hpt = tn // HEAD_DIM
    return pl.pallas_call(
        functools.partial(_proj_heads_kernel, q_tiles=n // 3 // tn, q_scale=q_scale),
        grid=(t // tm, n // tn),
        in_specs=[pl.BlockSpec((tm, k), lambda i, j: (i, 0)),
                  pl.BlockSpec((None, k, tn), lambda i, j: (layer, 0, j))],
        out_specs=pl.BlockSpec((hpt, tm, HEAD_DIM), lambda i, j: (j, i, 0)),
        out_shape=jax.ShapeDtypeStruct((n // HEAD_DIM, t, HEAD_DIM), BF16),
        scratch_shapes=[pltpu.VMEM((tm, k), BF16)],
        compiler_params=pltpu.CompilerParams(
            dimension_semantics=("parallel", "arbitrary"),
            vmem_limit_bytes=_vmem_limit(2 * tm * k * 4, tm * k * 2, 2 * k * tn * 2, 2 * tm * tn * 2, tm * tn * 4)),
        name="sb_qkv_proj",
    )(x, w)


def _sb_attn_kernel(q_ref, k_ref, v_ref, tri_ref, o_ref, acc_ref, run_ref):
    n_heads, tq, _ = q_ref.shape
    tk = SB_KEY_BLOCK
    diag_blocks = tq // tk
    i = pl.program_id(2)
    acc_ref[...] = jnp.zeros_like(acc_ref)
    run_ref[...] = jnp.zeros_like(run_ref)

    def visit(k0, row0, n_blocks, masked):
        heads = range(n_heads)
        zs, costs = [], []
        for h in heads:
            q = q_ref[h, row0:, :]
            kblk = k_ref[h, pl.ds(k0, n_blocks * tk), :]
            zs.append(lax.dot_general(q, kblk, (((1,), (1,)), ((), ())), preferred_element_type=F32))
        if masked:
            qpos = i * tq + row0 + lax.broadcasted_iota(jnp.int32, zs[0].shape, 0)
            kpos = k0 + lax.broadcasted_iota(jnp.int32, zs[0].shape, 1)
            causal = kpos < qpos
        for z in zs:
            neg_abs = pltpu.bitcast(pltpu.bitcast(z, jnp.uint32) | jnp.uint32(0x80000000), F32)
            cost = jnp.maximum(z, 0.0) + jnp.log(1.0 + jnp.exp2(neg_abs)) * LOG2_E
            if masked:
                cost = jnp.where(causal, cost, 0.0)
            costs.append(cost.astype(BF16))
        runs = [run_ref[h, row0:, :] for h in heads]
        pieces = [[None] * n_blocks for _ in heads]
        for s in reversed(range(n_blocks)):
            cols = slice(s * tk, (s + 1) * tk)
            for h in heads:
                incl = jnp.dot(costs[h][:, cols], tri_ref[...], preferred_element_type=F32)
                pieces[h][s] = jnp.exp2(zs[h][:, cols] - incl - runs[h])
                runs[h] = runs[h] + incl[:, 0:1]
        for h in heads:
            w = pieces[h][0] if n_blocks == 1 else jnp.concatenate(pieces[h], axis=1)
            if masked:
                w = jnp.where(causal, w, 0.0)
            vblk = v_ref[h, pl.ds(k0, n_blocks * tk), :]
            acc_ref[h, row0:, :] += jnp.dot(w.astype(BF16), vblk, preferred_element_type=F32)
            run_ref[h, row0:, :] = runs[h]

    diag0 = pl.multiple_of(i * tq, tq)
    for d in reversed(range(diag_blocks)):
        visit(diag0 + d * tk, d * tk, 1, True)

    def body(n, carry):
        visit(pl.multiple_of((i - 1 - n) * tq, tq), 0, diag_blocks, False)
        return carry

    lax.fori_loop(0, i, body, 0)
    o_ref[...] = acc_ref[...].astype(o_ref.dtype)


def _sb_attention(qkv, batch, seq, n_heads):
    tq, tk = min(SB_Q_ROWS, seq), SB_KEY_BLOCK
    nq = seq // tq
    hp = SB_HEADS_PER_STEP
    ng = n_heads // hp
    tri = (lax.broadcasted_iota(jnp.int32, (tk, tk), 0) >= lax.broadcasted_iota(jnp.int32, (tk, tk), 1)).astype(BF16)
    return pl.pallas_call(
        _sb_attn_kernel,
        grid=(ng, batch, nq),
        in_specs=[pl.BlockSpec((hp, tq, HEAD_DIM), lambda g, b, i: (g, b * nq + i, 0)),
                  pl.BlockSpec((hp, seq, HEAD_DIM), lambda g, b, i: (ng + g, b, 0)),
                  pl.BlockSpec((hp, seq, HEAD_DIM), lambda g, b, i: (2 * ng + g, b, 0)),
                  pl.BlockSpec((tk, tk), lambda g, b, i: (0, 0))],
        out_specs=pl.BlockSpec((hp, tq, HEAD_DIM), lambda g, b, i: (g, b * nq + i, 0)),
        out_shape=jax.ShapeDtypeStruct((n_heads, batch * seq, HEAD_DIM), BF16),
        scratch_shapes=[pltpu.VMEM((hp, tq, HEAD_DIM), F32), pltpu.VMEM((hp, tq, 1), F32)],
        compiler_params=pltpu.CompilerParams(
            dimension_semantics=("parallel", "parallel", "arbitrary"),
            vmem_limit_bytes=_vmem_limit(4 * hp * seq * HEAD_DIM * 2, 8 * hp * tq * tq * 4)),
        name="sb_attention",
    )(qkv, qkv, qkv, tri)


def _project_residual_ln(lhs, w_ref, x_ref, g_ref, b_ref, y_ref, alpha):
    rows = min(LN_CHUNK_ROWS, lhs.shape[0])
    for c in range(lhs.shape[0] // rows):
        sl = slice(c * rows, (c + 1) * rows)
        mixed = jnp.dot(lhs[sl], w_ref[...], preferred_element_type=F32)
        y_ref[sl, :] = _layer_norm_rows(alpha * x_ref[sl, :] + mixed, g_ref[...], b_ref[...])


def _outproj_ln_kernel(o_ref, w_ref, x_ref, g_ref, b_ref, y_ref, *, alpha):
    lhs = jnp.concatenate([o_ref[h] for h in range(o_ref.shape[0])], axis=1)
    _project_residual_ln(lhs, w_ref, x_ref, g_ref, b_ref, y_ref, alpha)


def _outproj_ln(o, w, layer, x, g, b, alpha):
    n_heads, t, _ = o.shape
    _, k, d = w.shape
    tm = min(LN_ROWS, t)
    return pl.pallas_call(
        functools.partial(_outproj_ln_kernel, alpha=alpha),
        grid=(t // tm,),
        in_specs=[pl.BlockSpec((n_heads, tm, HEAD_DIM), lambda i: (0, i, 0)),
                  pl.BlockSpec((None, k, d), lambda i: (layer, 0, 0), pipeline_mode=pl.Buffered(1)),
                  pl.BlockSpec((tm, d), lambda i: (i, 0)),
                  pl.BlockSpec((1, d), lambda i: (0, 0)),
                  pl.BlockSpec((1, d), lambda i: (0, 0))],
        out_specs=pl.BlockSpec((tm, d), lambda i: (i, 0)),
        out_shape=jax.ShapeDtypeStruct((t, d), F32),
        compiler_params=pltpu.CompilerParams(
            dimension_semantics=("parallel",),
            vmem_limit_bytes=_vmem_limit(2 * tm * k * 2, k * d * 2, 4 * tm * d * 4, 2 * tm * d * 4)),
        name="sb_outproj_ln",
    )(o, w, x, g, b)


def _ffn_up_kernel(x_ref, wg_ref, wu_ref, h_ref, xb_ref):
    @pl.when(pl.program_id(1) == 0)
    def _():
        xb_ref[...] = x_ref[...].astype(BF16)

    xb = xb_ref[...]
    gate = jnp.dot(xb, wg_ref[...], preferred_element_type=F32)
    up = jnp.dot(xb, wu_ref[...], preferred_element_type=F32)
    h_ref[...] = (jax.nn.silu(gate) * up).astype(h_ref.dtype)


def _ffn_up(x, w_gate_up, layer):
    t, d = x.shape
    f = w_gate_up.shape[2] // 2
    tm, tn = min(PROJ_ROWS, t), min(FFN_COLS, f)
    nj = f // tn
    return pl.pallas_call(
        _ffn_up_kernel,
        grid=(t // tm, nj),
        in_specs=[pl.BlockSpec((tm, d), lambda i, j: (i, 0)),
                  pl.BlockSpec((None, d, tn), lambda i, j: (layer, 0, j)),
                  pl.BlockSpec((None, d, tn), lambda i, j: (layer, 0, nj + j))],
        out_specs=pl.BlockSpec((tm, tn), lambda i, j: (i, j)),
        out_shape=jax.ShapeDtypeStruct((t, f), BF16),
        scratch_shapes=[pltpu.VMEM((tm, d), BF16)],
        compiler_params=pltpu.CompilerParams(
            dimension_semantics=("parallel", "arbitrary"),
            vmem_limit_bytes=_vmem_limit(2 * tm * d * 4, tm * d * 2, 4 * d * tn * 2, 2 * tm * tn * 2, 3 * tm * tn * 4)),
        name="ffn_up",
    )(x, w_gate_up, w_gate_up)


def _ffn_down_ln_kernel(h_ref, w_ref, x_ref, g_ref, b_ref, y_ref, *, alpha):
    _project_residual_ln(h_ref[...], w_ref, x_ref, g_ref, b_ref, y_ref, alpha)


def _ffn_down_ln(h, w, layer, x, g, b, alpha):
    t, f = h.shape
    d = w.shape[2]
    tm = min(FFN_DOWN_ROWS, t)
    return pl.pallas_call(
        functools.partial(_ffn_down_ln_kernel, alpha=alpha),
        grid=(t // tm,),
        in_specs=[pl.BlockSpec((tm, f), lambda i: (i, 0)),
                  pl.BlockSpec((None, f, d), lambda i: (layer, 0, 0), pipeline_mode=pl.Buffered(1)),
                  pl.BlockSpec((tm, d), lambda i: (i, 0)),
                  pl.BlockSpec((1, d), lambda i: (0, 0)),
                  pl.BlockSpec((1, d), lambda i: (0, 0))],
        out_specs=pl.BlockSpec((tm, d), lambda i: (i, 0)),
        out_shape=jax.ShapeDtypeStruct((t, d), F32),
        compiler_params=pltpu.CompilerParams(
            dimension_semantics=("parallel",),
            vmem_limit_bytes=_vmem_limit(f * d * 2, 2 * tm * f * 2, 4 * tm * d * 4, 2 * tm * d * 4)),
        name="ffn_down_ln",
    )(h, w, x, g, b)


def _rope_tables(seq):
    pos = jnp.arange(seq, dtype=F32)
    inv_freq = ROPE_THETA ** (-jnp.arange(0, HEAD_DIM, 2, dtype=F32) / HEAD_DIM)
    ang = pos[:, None] * inv_freq[None, :]
    cos, sin = jnp.cos(ang), jnp.sin(ang)
    return jnp.concatenate([cos, cos], axis=1), jnp.concatenate([-sin, sin], axis=1)


def _dil_proj_kernel(x_ref, wq_ref, wk_ref, wv_ref, cos_ref, sin_ref, q_ref, k_ref, v_ref, xp_ref, *planes):
    _, r, tu, _ = q_ref.shape
    if r == 1:
        xp_ref[...] = x_ref[...].astype(BF16)
    else:
        planes_ref, = planes
        for j in range(planes_ref.shape[0]):
            planes_ref[j] = x_ref[:, j * LANES:(j + 1) * LANES]
        for c in range(r):
            for j in range(planes_ref.shape[0]):
                rows = planes_ref[j, pl.ds(c, tu, stride=r), :]
                xp_ref[c * tu:(c + 1) * tu, j * LANES:(j + 1) * LANES] = rows.astype(BF16)
    xb = xp_ref[...]
    cos, sin = cos_ref[...], sin_ref[...]
    for w_ref, out_ref, rope in ((wq_ref, q_ref, True), (wk_ref, k_ref, True), (wv_ref, v_ref, False)):
        acc = jnp.dot(xb, w_ref[...], preferred_element_type=F32)
        for h in range(out_ref.shape[0]):
            a = acc[:, h * HEAD_DIM:(h + 1) * HEAD_DIM]
            if rope:
                a = a * cos + pltpu.roll(a, HEAD_DIM // 2, 1) * sin
            a = a.astype(out_ref.dtype)
            for c in range(r):
                out_ref[h, c] = a[c * tu:(c + 1) * tu, :]


def _dil_proj(x, w, layer, group, dilation, cos2, sin2, batch):
    t, d = x.shape
    s = t // batch
    r = dilation
    length = s // r
    tm = min(DIL_PROJ_ROWS, s)
    tu = tm // r
    nt = s // tm
    planes = [] if r == 1 else [pltpu.VMEM((d // LANES, tm, LANES), F32)]
    n_groups = len(DIL_CONFIGS)
    gw = DIL_HEADS_PER_GROUP * HEAD_DIM
    cos_p = cos2.reshape(nt, tu, r, HEAD_DIM).transpose(0, 2, 1, 3).reshape(s, HEAD_DIM)
    sin_p = sin2.reshape(nt, tu, r, HEAD_DIM).transpose(0, 2, 1, 3).reshape(s, HEAD_DIM)
    w_spec = lambda kind: pl.BlockSpec((None, d, gw), lambda i: (layer, 0, kind * n_groups + group),
                                       pipeline_mode=pl.Buffered(1))
    tab_spec = pl.BlockSpec((tm, HEAD_DIM), lambda i: (i % nt, 0))
    out_spec = pl.BlockSpec((DIL_HEADS_PER_GROUP, None, r, tu, HEAD_DIM), lambda i: (0, i // nt, 0, i % nt, 0))
    out_sds = jax.ShapeDtypeStruct((DIL_HEADS_PER_GROUP, batch, r, length, HEAD_DIM), BF16)
    return pl.pallas_call(
        _dil_proj_kernel,
        grid=(t // tm,),
        in_specs=[pl.BlockSpec((tm, d), lambda i: (i, 0)),
                  w_spec(0), w_spec(1), w_spec(2), tab_spec, tab_spec],
        out_specs=[out_spec, out_spec, out_spec],
        out_shape=[out_sds, out_sds, out_sds],
        scratch_shapes=[pltpu.VMEM((tm, d), BF16)] + planes,
        compiler_params=pltpu.CompilerParams(
            dimension_semantics=("parallel",),
            vmem_limit_bytes=_vmem_limit(3 * tm * d * 4, tm * d * 2, 3 * d * gw * 2, 6 * tm * gw * 2, 2 * tm * gw * 4)),
        name=f"dil_qkv_proj_r{r}",
    )(x, w, w, w, cos_p, sin_p)


def _dil_attn_kernel(q_ref, k_ref, v_ref, kp_ref, vp_ref, o_ref, lse_ref, *, scale):
    n_heads, tu, _ = q_ref.shape
    sub = DIL_SUB
    iu = pl.program_id(2)
    qi = lax.broadcasted_iota(jnp.int32, (sub, 2 * sub), 0)
    kj = lax.broadcasted_iota(jnp.int32, (sub, 2 * sub), 1)
    in_window = (kj >= qi) & (kj <= qi + sub)
    lane = lax.broadcasted_iota(jnp.int32, (sub, LANES), 1)
    for s in range(tu // sub):
        rows = slice(s * sub, (s + 1) * sub)
        if s == 0:
            valid = in_window & (kj >= jnp.where(iu > 0, 0, sub))
        else:
            valid = in_window
        lse_tile = jnp.zeros((sub, LANES), F32)
        for h in range(n_heads):
            q = q_ref[h, rows, :]
            if s == 0:
                k_prev, v_prev = kp_ref[h], vp_ref[h]
            else:
                k_prev, v_prev = k_ref[h, (s - 1) * sub:s * sub, :], v_ref[h, (s - 1) * sub:s * sub, :]
            keys = jnp.concatenate([k_prev, k_ref[h, rows, :]], axis=0)
            vals = jnp.concatenate([v_prev, v_ref[h, rows, :]], axis=0)
            sc = lax.dot_general(q, keys, (((1,), (1,)), ((), ())), preferred_element_type=F32) * scale
            sc = jnp.where(valid, sc, -jnp.inf)
            m = jnp.max(sc, axis=1, keepdims=True)
            p = jnp.exp(sc - m)
            den = jnp.sum(p, axis=1, keepdims=True)
            o = jnp.dot(p.astype(BF16), vals, preferred_element_type=F32) / den
            o_ref[rows, h * HEAD_DIM:(h + 1) * HEAD_DIM] = o.astype(o_ref.dtype)
            lse_tile = jnp.where(lane == h, m + jnp.log(den), lse_tile)
        lse_ref[rows, :] = lse_tile


def _dil_attention(q, k, v):
    n_heads, b, r, length, _ = q.shape
    tu = min(DIL_Q_ROWS, length)
    sub = DIL_SUB
    gw = n_heads * HEAD_DIM
    cur = pl.BlockSpec((n_heads, None, None, tu, HEAD_DIM), lambda bi, c, iu: (0, bi, c, iu, 0))
    prev = pl.BlockSpec((n_heads, None, None, sub, HEAD_DIM),
                        lambda bi, c, iu: (0, bi, c, jnp.maximum(iu * (tu // sub) - 1, 0), 0))
    return pl.pallas_call(
        functools.partial(_dil_attn_kernel, scale=1.0 / math.sqrt(HEAD_DIM)),
        grid=(b, r, length // tu),
        in_specs=[cur, cur, cur, prev, prev],
        out_specs=[pl.BlockSpec((None, None, tu, gw), lambda bi, c, iu: (bi, c, iu, 0)),
                   pl.BlockSpec((None, None, tu, LANES), lambda bi, c, iu: (bi, c, iu, 0))],
        out_shape=[jax.ShapeDtypeStruct((b, r, length, gw), BF16),
                   jax.ShapeDtypeStruct((b, r, length, LANES), F32)],
        compiler_params=pltpu.CompilerParams(
            dimension_semantics=("parallel", "parallel", "parallel"),
            vmem_limit_bytes=_vmem_limit(6 * tu * gw * 2, 4 * sub * gw * 2, 2 * tu * gw * 2, 2 * tu * LANES * 4)),
        name=f"dil_attention_r{r}",
    )(q, k, v, k, v)


def _dil_outproj_ln_kernel(o0_ref, o1_ref, o2_ref, l0_ref, l1_ref, l2_ref, w_ref, x_ref, g_ref, b_ref, y_ref,
                           planes_ref, *, alpha):
    o_refs, l_refs = (o0_ref, o1_ref, o2_ref), (l0_ref, l1_ref, l2_ref)
    n_o = len(o_refs) * DIL_HEADS_PER_GROUP

    def natural(ref, cols, plane):
        r, tu, _ = ref.shape
        if r == 1:
            return ref[0, :, cols].astype(F32)
        for c in range(r):
            planes_ref[plane, pl.ds(c, tu, stride=r), :] = ref[c, :, cols].astype(F32)
        return planes_ref[plane]

    lses = [natural(l_ref, slice(0, LANES), n_o + g) for g, l_ref in enumerate(l_refs)]
    mx = jnp.maximum(jnp.maximum(lses[0], lses[1]), lses[2])
    es = [jnp.exp(l - mx) for l in lses]
    den = es[0] + es[1] + es[2]
    pieces = []
    for g, (o_ref, e) in enumerate(zip(o_refs, es)):
        share = e / den
        for h in range(DIL_HEADS_PER_GROUP):
            o = natural(o_ref, slice(h * HEAD_DIM, (h + 1) * HEAD_DIM), g * DIL_HEADS_PER_GROUP + h)
            pieces.append((o * share[:, h:h + 1]).astype(BF16))
    _project_residual_ln(jnp.concatenate(pieces, axis=1), w_ref, x_ref, g_ref, b_ref, y_ref, alpha)


def _dil_outproj_ln(os, lses, w, layer, x, g, b, alpha):
    t, d = x.shape
    _, k, _ = w.shape
    batch = os[0].shape[0]
    tm = min(LN_ROWS, t // batch)
    nt = t // batch // tm

    def res_spec(a):
        _, r, _, width = a.shape
        return pl.BlockSpec((None, r, tm // r, width), lambda i: (i // nt, 0, i % nt, 0))

    n_planes = len(os) * DIL_HEADS_PER_GROUP + len(lses)
    return pl.pallas_call(
        functools.partial(_dil_outproj_ln_kernel, alpha=alpha),
        grid=(t // tm,),
        in_specs=[res_spec(a) for a in (*os, *lses)] + [
                  pl.BlockSpec((None, k, d), lambda i: (layer, 0, 0), pipeline_mode=pl.Buffered(1)),
                  pl.BlockSpec((tm, d), lambda i: (i, 0)),
                  pl.BlockSpec((1, d), lambda i: (0, 0)),
                  pl.BlockSpec((1, d), lambda i: (0, 0))],
        out_specs=pl.BlockSpec((tm, d), lambda i: (i, 0)),
        out_shape=jax.ShapeDtypeStruct((t, d), F32),
        scratch_shapes=[pltpu.VMEM((n_planes, tm, LANES), F32)],
        compiler_params=pltpu.CompilerParams(
            dimension_semantics=("parallel",),
            vmem_limit_bytes=_vmem_limit(3 * tm * k * 2, k * d * 2, 4 * tm * d * 4, 2 * tm * d * 4,
                                         n_planes * tm * LANES * 4)),
        name="dil_outproj_ln",
    )(*os, *lses, w, x, g, b)


def kernel(x, w_qkv_sb, w_o_sb, w_qkv_dil, w_o_dil, ln_mix_g, ln_mix_b, w_gate_up, w_down, ln_ffn_g, ln_ffn_b):
    batch, seq, d_model = x.shape
    depth = ln_mix_g.shape[0]
    alpha = (2.0 * depth) ** 0.25
    t = batch * seq
    sb_heads = w_o_sb.shape[1] // HEAD_DIM
    assert all(window // dilation == DIL_SUB for window, dilation in DIL_CONFIGS)

    w_qkv_sb, w_o_sb, w_qkv_dil, w_o_dil, w_gate_up, w_down = (
        w.astype(BF16) for w in (w_qkv_sb, w_o_sb, w_qkv_dil, w_o_dil, w_gate_up, w_down))
    cos2, sin2 = _rope_tables(seq)

    xf = x.reshape(t, d_model)
    for layer in range(depth):
        j = layer // 2
        g_mix, b_mix = ln_mix_g[layer][None, :], ln_mix_b[layer][None, :]
        if layer % 2 == 0:
            qkv = _proj_heads(xf, w_qkv_sb, j, LOG2_E / math.sqrt(HEAD_DIM))
            o = _sb_attention(qkv, batch, seq, sb_heads)
            xf = _outproj_ln(o, w_o_sb, j, xf, g_mix, b_mix, alpha)
        else:
            os, lses = [], []
            for group, (_, dilation) in enumerate(DIL_CONFIGS):
                q, k, v = _dil_proj(xf, w_qkv_dil, j, group, dilation, cos2, sin2, batch)
                o, lse = _dil_attention(q, k, v)
                os.append(o)
                lses.append(lse)
            xf = _dil_outproj_ln(os, lses, w_o_dil, j, xf, g_mix, b_mix, alpha)
        h = _ffn_up(xf, w_gate_up, layer)
        xf = _ffn_down_ln(h, w_down, layer, xf, ln_ffn_g[layer][None, :], ln_ffn_b[layer][None, :], alpha)
    return xf.reshape(batch, seq, d_model)
```

```python
import functools
import math

import jax
import jax.numpy as jnp
from jax import lax
from jax.experimental import pallas as pl
from jax.experimental.pallas import tpu as pltpu

HEAD_DIM = 128
DIL_CONFIGS = ((128, 1), (512, 4), (2048, 16))
DIL_HEADS_PER_GROUP = 6
ROPE_THETA = 10000.0
LN_EPS = 1e-5
LOG2_E = math.log2(math.e)

V7X_VMEM_BYTES = 64 * 1024 * 1024
V7X_MXU_DIM = 256
LANES = 128
BF16_SUBLANES = 16

PROJ_ROWS = 1024
PROJ_COLS = 1024
LN_ROWS = 512
LN_CHUNK_ROWS = 128
FFN_COLS = 512
FFN_DOWN_ROWS = 256
SB_Q_ROWS = 512
SB_KEY_BLOCK = V7X_MXU_DIM
SB_HEADS_PER_STEP = 4
DIL_PROJ_ROWS = 512
DIL_Q_ROWS = 256
DIL_SUB = 128

BF16 = jnp.bfloat16
F32 = jnp.float32


def _vmem_limit(*nbytes):
    return int(min(V7X_VMEM_BYTES * 7 // 8, max(sum(nbytes) * 3 // 2, V7X_VMEM_BYTES // 4)))


def _layer_norm_rows(y, g, b):
    mu = jnp.mean(y, axis=-1, keepdims=True)
    d = y - mu
    var = jnp.mean(d * d, axis=-1, keepdims=True)
    return d * lax.rsqrt(var + LN_EPS) * g + b


def _proj_heads_kernel(x_ref, w_ref, o_ref, xb_ref, *, q_tiles, q_scale):
    j = pl.program_id(1)

    @pl.when(j == 0)
    def _():
        xb_ref[...] = x_ref[...].astype(BF16)

    acc = jnp.dot(xb_ref[...], w_ref[...], preferred_element_type=F32)
    acc = acc * jnp.where(j < q_tiles, q_scale, 1.0)
    for h in range(o_ref.shape[0]):
        o_ref[h] = acc[:, h * HEAD_DIM:(h + 1) * HEAD_DIM].astype(o_ref.dtype)


def _proj_heads(x, w, layer, q_scale):
    t, k = x.shape
    n = w.shape[2]
    tm, tn = min(PROJ_ROWS, t), min(PROJ_COLS, n // 3)
    hpt = tn // HEAD_DIM
    return pl.pallas_call(
        functools.partial(_proj_heads_kernel, q_tiles=n // 3 // tn, q_scale=q_scale),
        grid=(t // tm, n // tn),
        in_specs=[pl.BlockSpec((tm, k), lambda i, j: (i, 0)),
                  pl.BlockSpec((None, k, tn), lambda i, j: (layer, 0, j))],
        out_specs=pl.BlockSpec((hpt, tm, HEAD_DIM), lambda i, j: (j, i, 0)),
        out_shape=jax.ShapeDtypeStruct((n // HEAD_DIM, t, HEAD_DIM), BF16),
        scratch_shapes=[pltpu.VMEM((tm, k), BF16)],
        compiler_params=pltpu.CompilerParams(
            dimension_semantics=("parallel", "arbitrary"),
            vmem_limit_bytes=_vmem_limit(2 * tm * k * 4, tm * k * 2, 2 * k * tn * 2, 2 * tm * tn * 2, tm * tn * 4)),
        name="sb_qkv_proj",
    )(x, w)


def _sb_attn_kernel(q_ref, k_ref, v_ref, tri_ref, *rest):
    n_cast = (len(rest) - 3) // 2
    cast_in, o_ref, cast_out = rest[:n_cast], rest[n_cast], rest[n_cast + 1:2 * n_cast + 1]
    acc_ref, run_ref = rest[2 * n_cast + 1:]
    for src, dst in zip(cast_in, cast_out):
        dst[...] = src[...].astype(dst.dtype)
    n_heads, tq, _ = q_ref.shape
    tk = SB_KEY_BLOCK
    diag_blocks = tq // tk
    i = pl.program_id(2)
    acc_ref[...] = jnp.zeros_like(acc_ref)
    run_ref[...] = jnp.zeros_like(run_ref)

    def visit(k0, row0, n_blocks, masked):
        heads = range(n_heads)
        zs, costs = [], []
        for h in heads:
            q = q_ref[h, row0:, :]
            kblk = k_ref[h, pl.ds(k0, n_blocks * tk), :]
            zs.append(lax.dot_general(q, kblk, (((1,), (1,)), ((), ())), preferred_element_type=F32))
        if masked:
            qpos = i * tq + row0 + lax.broadcasted_iota(jnp.int32, zs[0].shape, 0)
            kpos = k0 + lax.broadcasted_iota(jnp.int32, zs[0].shape, 1)
            causal = kpos < qpos
        for z in zs:
            neg_abs = pltpu.bitcast(pltpu.bitcast(z, jnp.uint32) | jnp.uint32(0x80000000), F32)
            cost = jnp.maximum(z, 0.0) + jnp.log(1.0 + jnp.exp2(neg_abs)) * LOG2_E
            if masked:
                cost = jnp.where(causal, cost, 0.0)
            costs.append(cost.astype(BF16))
        runs = [run_ref[h, row0:, :] for h in heads]
        pieces = [[None] * n_blocks for _ in heads]
        for s in reversed(range(n_blocks)):
            cols = slice(s * tk, (s + 1) * tk)
            for h in heads:
                incl = jnp.dot(costs[h][:, cols], tri_ref[...], preferred_element_type=F32)
                pieces[h][s] = jnp.exp2(zs[h][:, cols] - incl - runs[h])
                runs[h] = runs[h] + incl[:, 0:1]
        for h in heads:
            w = pieces[h][0] if n_blocks == 1 else jnp.concatenate(pieces[h], axis=1)
            if masked:
                w = jnp.where(causal, w, 0.0)
            vblk = v_ref[h, pl.ds(k0, n_blocks * tk), :]
            acc_ref[h, row0:, :] += jnp.dot(w.astype(BF16), vblk, preferred_element_type=F32)
            run_ref[h, row0:, :] = runs[h]

    diag0 = pl.multiple_of(i * tq, tq)
    for d in reversed(range(diag_blocks)):
        visit(diag0 + d * tk, d * tk, 1, True)

    def body(n, carry):
        visit(pl.multiple_of((i - 1 - n) * tq, tq), 0, diag_blocks, False)
        return carry

    lax.fori_loop(0, i, body, 0)
    o_ref[...] = acc_ref[...].astype(o_ref.dtype)


def _cast_slab_rows(rows, n_steps):
    slab = -(-rows // n_steps)
    slab += -slab % BF16_SUBLANES
    while rows % slab:
        slab += BF16_SUBLANES
    return slab


def _sb_attention(qkv, batch, seq, n_heads, weights=()):
    tq, tk = min(SB_Q_ROWS, seq), SB_KEY_BLOCK
    nq = seq // tq
    hp = SB_HEADS_PER_STEP
    ng = n_heads // hp
    n_steps = ng * batch * nq
    tri = (lax.broadcasted_iota(jnp.int32, (tk, tk), 0) >= lax.broadcasted_iota(jnp.int32, (tk, tk), 1)).astype(BF16)
    flat = [w.reshape(-1, w.shape[-1]) for w in weights]
    slab_specs, slab_bytes = [], 0
    for w in flat:
        rows, cols = w.shape
        slab = _cast_slab_rows(rows, n_steps)
        last = rows // slab - 1
        slab_specs.append(pl.BlockSpec(
            (slab, cols), lambda g, b, i, last=last: (jnp.minimum((g * batch + b) * nq + i, last), 0)))
        slab_bytes += 2 * slab * cols * (4 + 2)
    outs = pl.pallas_call(
        _sb_attn_kernel,
        grid=(ng, batch, nq),
        in_specs=[pl.BlockSpec((hp, tq, HEAD_DIM), lambda g, b, i: (g, b * nq + i, 0)),
                  pl.BlockSpec((hp, seq, HEAD_DIM), lambda g, b, i: (ng + g, b, 0)),
                  pl.BlockSpec((hp, seq, HEAD_DIM), lambda g, b, i: (2 * ng + g, b, 0)),
                  pl.BlockSpec((tk, tk), lambda g, b, i: (0, 0))] + slab_specs,
        out_specs=[pl.BlockSpec((hp, tq, HEAD_DIM), lambda g, b, i: (g, b * nq + i, 0))] + slab_specs,
        out_shape=[jax.ShapeDtypeStruct((n_heads, batch * seq, HEAD_DIM), BF16)] +
                  [jax.ShapeDtypeStruct(w.shape, BF16) for w in flat],
        scratch_shapes=[pltpu.VMEM((hp, tq, HEAD_DIM), F32), pltpu.VMEM((hp, tq, 1), F32)],
        compiler_params=pltpu.CompilerParams(
            dimension_semantics=("arbitrary", "arbitrary", "arbitrary"),
            vmem_limit_bytes=_vmem_limit(4 * hp * seq * HEAD_DIM * 2, 8 * hp * tq * tq * 4, slab_bytes)),
        name="sb_attention",
    )(qkv, qkv, qkv, tri, *flat)
    return outs[0], [o.reshape(w.shape) for o, w in zip(outs[1:], weights)]


def _project_residual_ln(lhs, w_ref, x_ref, g_ref, b_ref, y_ref, alpha):
    rows = min(LN_CHUNK_ROWS, lhs.shape[0])
    for c in range(lhs.shape[0] // rows):
        sl = slice(c * rows, (c + 1) * rows)
        mixed = jnp.dot(lhs[sl], w_ref[...], preferred_element_type=F32)
        y_ref[sl, :] = _layer_norm_rows(alpha * x_ref[sl, :] + mixed, g_ref[...], b_ref[...])


def _outproj_ln_kernel(o_ref, w_ref, x_ref, g_ref, b_ref, y_ref, *, alpha):
    lhs = jnp.concatenate([o_ref[h] for h in range(o_ref.shape[0])], axis=1)
    _project_residual_ln(lhs, w_ref, x_ref, g_ref, b_ref, y_ref, alpha)


def _outproj_ln(o, w, layer, x, g, b, alpha):
    n_heads, t, _ = o.shape
    _, k, d = w.shape
    tm = min(LN_ROWS, t)
    return pl.pallas_call(
        functools.partial(_outproj_ln_kernel, alpha=alpha),
        grid=(t // tm,),
        in_specs=[pl.BlockSpec((n_heads, tm, HEAD_DIM), lambda i: (0, i, 0)),
                  pl.BlockSpec((None, k, d), lambda i: (layer, 0, 0), pipeline_mode=pl.Buffered(1)),
                  pl.BlockSpec((tm, d), lambda i: (i, 0)),
                  pl.BlockSpec((1, d), lambda i: (0, 0)),
                  pl.BlockSpec((1, d), lambda i: (0, 0))],
        out_specs=pl.BlockSpec((tm, d), lambda i: (i, 0)),
        out_shape=jax.ShapeDtypeStruct((t, d), F32),
        compiler_params=pltpu.CompilerParams(
            dimension_semantics=("parallel",),
            vmem_limit_bytes=_vmem_limit(2 * tm * k * 2, k * d * 2, 4 * tm * d * 4, 2 * tm * d * 4)),
        name="sb_outproj_ln",
    )(o, w, x, g, b)


def _ffn_up_kernel(x_ref, wg_ref, wu_ref, h_ref, xb_ref):
    @pl.when(pl.program_id(1) == 0)
    def _():
        xb_ref[...] = x_ref[...].astype(BF16)

    xb = xb_ref[...]
    gate = jnp.dot(xb, wg_ref[...], preferred_element_type=F32)
    up = jnp.dot(xb, wu_ref[...], preferred_element_type=F32)
    h_ref[...] = (jax.nn.silu(gate) * up).astype(h_ref.dtype)


def _ffn_up(x, w_gate_up, layer):
    t, d = x.shape
    f = w_gate_up.shape[2] // 2
    tm, tn = min(PROJ_ROWS, t), min(FFN_COLS, f)
    nj = f // tn
    return pl.pallas_call(
        _ffn_up_kernel,
        grid=(t // tm, nj),
        in_specs=[pl.BlockSpec((tm, d), lambda i, j: (i, 0)),
                  pl.BlockSpec((None, d, tn), lambda i, j: (layer, 0, j)),
                  pl.BlockSpec((None, d, tn), lambda i, j: (layer, 0, nj + j))],
        out_specs=pl.BlockSpec((tm, tn), lambda i, j: (i, j)),
        out_shape=jax.ShapeDtypeStruct((t, f), BF16),
        scratch_shapes=[pltpu.VMEM((tm, d), BF16)],
        compiler_params=pltpu.CompilerParams(
            dimension_semantics=("parallel", "arbitrary"),
            vmem_limit_bytes=_vmem_limit(2 * tm * d * 4, tm * d * 2, 4 * d * tn * 2, 2 * tm * tn * 2, 3 * tm * tn * 4)),
        name="ffn_up",
    )(x, w_gate_up, w_gate_up)


def _ffn_down_ln_kernel(h_ref, w_ref, x_ref, g_ref, b_ref, y_ref, *, alpha):
    _project_residual_ln(h_ref[...], w_ref, x_ref, g_ref, b_ref, y_ref, alpha)


def _ffn_down_ln(h, w, layer, x, g, b, alpha):
    t, f = h.shape
    d = w.shape[2]
    tm = min(FFN_DOWN_ROWS, t)
    return pl.pallas_call(
        functools.partial(_ffn_down_ln_kernel, alpha=alpha),
        grid=(t // tm,),
        in_specs=[pl.BlockSpec((tm, f), lambda i: (i, 0)),
                  pl.BlockSpec((None, f, d), lambda i: (layer, 0, 0), pipeline_mode=pl.Buffered(1)),
                  pl.BlockSpec((tm, d), lambda i: (i, 0)),
                  pl.BlockSpec((1, d), lambda i: (0, 0)),
                  pl.BlockSpec((1, d), lambda i: (0, 0))],
        out_specs=pl.BlockSpec((tm, d), lambda i: (i, 0)),
        out_shape=jax.ShapeDtypeStruct((t, d), F32),
        compiler_params=pltpu.CompilerParams(
            dimension_semantics=("parallel",),
            vmem_limit_bytes=_vmem_limit(f * d * 2, 2 * tm * f * 2, 4 * tm * d * 4, 2 * tm * d * 4)),
        name="ffn_down_ln",
    )(h, w, x, g, b)


def _rope_tables(seq):
    pos = jnp.arange(seq, dtype=F32)
    inv_freq = ROPE_THETA ** (-jnp.arange(0, HEAD_DIM, 2, dtype=F32) / HEAD_DIM)
    ang = pos[:, None] * inv_freq[None, :]
    cos, sin = jnp.cos(ang), jnp.sin(ang)
    return jnp.concatenate([cos, cos], axis=1), jnp.concatenate([-sin, sin], axis=1)


def _dil_proj_kernel(x_ref, wq_ref, wk_ref, wv_ref, cos_ref, sin_ref, q_ref, k_ref, v_ref, xp_ref, *planes):
    _, r, tu, _ = q_ref.shape
    if r == 1:
        xp_ref[...] = x_ref[...].astype(BF16)
    else:
        planes_ref, = planes
        for j in range(planes_ref.shape[0]):
            planes_ref[j] = x_ref[:, j * LANES:(j + 1) * LANES]
        for c in range(r):
            for j in range(planes_ref.shape[0]):
                rows = planes_ref[j, pl.ds(c, tu, stride=r), :]
                xp_ref[c * tu:(c + 1) * tu, j * LANES:(j + 1) * LANES] = rows.astype(BF16)
    xb = xp_ref[...]
    cos, sin = cos_ref[...], sin_ref[...]
    for w_ref, out_ref, rope in ((wq_ref, q_ref, True), (wk_ref, k_ref, True), (wv_ref, v_ref, False)):
        acc = jnp.dot(xb, w_ref[...], preferred_element_type=F32)
        for h in range(out_ref.shape[0]):
            a = acc[:, h * HEAD_DIM:(h + 1) * HEAD_DIM]
            if rope:
                a = a * cos + pltpu.roll(a, HEAD_DIM // 2, 1) * sin
            a = a.astype(out_ref.dtype)
            for c in range(r):
                out_ref[h, c] = a[c * tu:(c + 1) * tu, :]


def _dil_proj(x, w, layer, group, dilation, cos2, sin2, batch):
    t, d = x.shape
    s = t // batch
    r = dilation
    length = s // r
    tm = min(DIL_PROJ_ROWS, s)
    tu = tm // r
    nt = s // tm
    planes = [] if r == 1 else [pltpu.VMEM((d // LANES, tm, LANES), F32)]
    n_groups = len(DIL_CONFIGS)
    gw = DIL_HEADS_PER_GROUP * HEAD_DIM
    cos_p = cos2.reshape(nt, tu, r, HEAD_DIM).transpose(0, 2, 1, 3).reshape(s, HEAD_DIM)
    sin_p = sin2.reshape(nt, tu, r, HEAD_DIM).transpose(0, 2, 1, 3).reshape(s, HEAD_DIM)
    w_spec = lambda kind: pl.BlockSpec((None, d, gw), lambda i: (layer, 0, kind * n_groups + group),
                                       pipeline_mode=pl.Buffered(1))
    tab_spec = pl.BlockSpec((tm, HEAD_DIM), lambda i: (i % nt, 0))
    out_spec = pl.BlockSpec((DIL_HEADS_PER_GROUP, None, r, tu, HEAD_DIM), lambda i: (0, i // nt, 0, i % nt, 0))
    out_sds = jax.ShapeDtypeStruct((DIL_HEADS_PER_GROUP, batch, r, length, HEAD_DIM), BF16)
    return pl.pallas_call(
        _dil_proj_kernel,
        grid=(t // tm,),
        in_specs=[pl.BlockSpec((tm, d), lambda i: (i, 0)),
                  w_spec(0), w_spec(1), w_spec(2), tab_spec, tab_spec],
        out_specs=[out_spec, out_spec, out_spec],
        out_shape=[out_sds, out_sds, out_sds],
        scratch_shapes=[pltpu.VMEM((tm, d), BF16)] + planes,
        compiler_params=pltpu.CompilerParams(
            dimension_semantics=("parallel",),
            vmem_limit_bytes=_vmem_limit(3 * tm * d * 4, tm * d * 2, 3 * d * gw * 2, 6 * tm * gw * 2, 2 * tm * gw * 4)),
        name=f"dil_qkv_proj_r{r}",
    )(x, w, w, w, cos_p, sin_p)


def _dil_attn_kernel(q_ref, k_ref, v_ref, kp_ref, vp_ref, o_ref, lse_ref, *, scale):
    n_heads, tu, _ = q_ref.shape
    sub = DIL_SUB
    iu = pl.program_id(2)
    qi = lax.broadcasted_iota(jnp.int32, (sub, 2 * sub), 0)
    kj = lax.broadcasted_iota(jnp.int32, (sub, 2 * sub), 1)
    in_window = (kj >= qi) & (kj <= qi + sub)
    lane = lax.broadcasted_iota(jnp.int32, (sub, LANES), 1)
    for s in range(tu // sub):
        rows = slice(s * sub, (s + 1) * sub)
        if s == 0:
            valid = in_window & (kj >= jnp.where(iu > 0, 0, sub))
        else:
            valid = in_window
        lse_tile = jnp.zeros((sub, LANES), F32)
        for h in range(n_heads):
            q = q_ref[h, rows, :]
            if s == 0:
                k_prev, v_prev = kp_ref[h], vp_ref[h]
            else:
                k_prev, v_prev = k_ref[h, (s - 1) * sub:s * sub, :], v_ref[h, (s - 1) * sub:s * sub, :]
            keys = jnp.concatenate([k_prev, k_ref[h, rows, :]], axis=0)
            vals = jnp.concatenate([v_prev, v_ref[h, rows, :]], axis=0)
            sc = lax.dot_general(q, keys, (((1,), (1,)), ((), ())), preferred_element_type=F32) * scale
            sc = jnp.where(valid, sc, -jnp.inf)
            m = jnp.max(sc, axis=1, keepdims=True)
            p = jnp.exp(sc - m)
            den = jnp.sum(p, axis=1, keepdims=True)
            o = jnp.dot(p.astype(BF16), vals, preferred_element_type=F32) / den
            o_ref[rows, h * HEAD_DIM:(h + 1) * HEAD_DIM] = o.astype(o_ref.dtype)
            lse_tile = jnp.where(lane == h, m + jnp.log(den), lse_tile)
        lse_ref[rows, :] = lse_tile


def _dil_attention(q, k, v):
    n_heads, b, r, length, _ = q.shape
    tu = min(DIL_Q_ROWS, length)
    sub = DIL_SUB
    gw = n_heads * HEAD_DIM
    cur = pl.BlockSpec((n_heads, None, None, tu, HEAD_DIM), lambda bi, c, iu: (0, bi, c, iu, 0))
    prev = pl.BlockSpec((n_heads, None, None, sub, HEAD_DIM),
                        lambda bi, c, iu: (0, bi, c, jnp.maximum(iu * (tu // sub) - 1, 0), 0))
    return pl.pallas_call(
        functools.partial(_dil_attn_kernel, scale=1.0 / math.sqrt(HEAD_DIM)),
        grid=(b, r, length // tu),
        in_specs=[cur, cur, cur, prev, prev],
        out_specs=[pl.BlockSpec((None, None, tu, gw), lambda bi, c, iu: (bi, c, iu, 0)),
                   pl.BlockSpec((None, None, tu, LANES), lambda bi, c, iu: (bi, c, iu, 0))],
        out_shape=[jax.ShapeDtypeStruct((b, r, length, gw), BF16),
                   jax.ShapeDtypeStruct((b, r, length, LANES), F32)],
        compiler_params=pltpu.CompilerParams(
            dimension_semantics=("parallel", "parallel", "parallel"),
            vmem_limit_bytes=_vmem_limit(6 * tu * gw * 2, 4 * sub * gw * 2, 2 * tu * gw * 2, 2 * tu * LANES * 4)),
        name=f"dil_attention_r{r}",
    )(q, k, v, k, v)


def _dil_outproj_ln_kernel(o0_ref, o1_ref, o2_ref, l0_ref, l1_ref, l2_ref, w_ref, x_ref, g_ref, b_ref, y_ref,
                           planes_ref, *, alpha):
    o_refs, l_refs = (o0_ref, o1_ref, o2_ref), (l0_ref, l1_ref, l2_ref)
    n_o = len(o_refs) * DIL_HEADS_PER_GROUP

    def natural(ref, cols, plane):
        r, tu, _ = ref.shape
        if r == 1:
            return ref[0, :, cols].astype(F32)
        for c in range(r):
            planes_ref[plane, pl.ds(c, tu, stride=r), :] = ref[c, :, cols].astype(F32)
        return planes_ref[plane]

    lses = [natural(l_ref, slice(0, LANES), n_o + g) for g, l_ref in enumerate(l_refs)]
    mx = jnp.maximum(jnp.maximum(lses[0], lses[1]), lses[2])
    es = [jnp.exp(l - mx) for l in lses]
    den = es[0] + es[1] + es[2]
    pieces = []
    for g, (o_ref, e) in enumerate(zip(o_refs, es)):
        share = e / den
        for h in range(DIL_HEADS_PER_GROUP):
            o = natural(o_ref, slice(h * HEAD_DIM, (h + 1) * HEAD_DIM), g * DIL_HEADS_PER_GROUP + h)
            pieces.append((o * share[:, h:h + 1]).astype(BF16))
    _project_residual_ln(jnp.concatenate(pieces, axis=1), w_ref, x_ref, g_ref, b_ref, y_ref, alpha)


def _dil_outproj_ln(os, lses, w, layer, x, g, b, alpha):
    t, d = x.shape
    _, k, _ = w.shape
    batch = os[0].shape[0]
    tm = min(LN_ROWS, t // batch)
    nt = t // batch // tm

    def res_spec(a):
        _, r, _, width = a.shape
        return pl.BlockSpec((None, r, tm // r, width), lambda i: (i // nt, 0, i % nt, 0))

    n_planes = len(os) * DIL_HEADS_PER_GROUP + len(lses)
    return pl.pallas_call(
        functools.partial(_dil_outproj_ln_kernel, alpha=alpha),
        grid=(t // tm,),
        in_specs=[res_spec(a) for a in (*os, *lses)] + [
                  pl.BlockSpec((None, k, d), lambda i: (layer, 0, 0), pipeline_mode=pl.Buffered(1)),
                  pl.BlockSpec((tm, d), lambda i: (i, 0)),
                  pl.BlockSpec((1, d), lambda i: (0, 0)),
                  pl.BlockSpec((1, d), lambda i: (0, 0))],
        out_specs=pl.BlockSpec((tm, d), lambda i: (i, 0)),
        out_shape=jax.ShapeDtypeStruct((t, d), F32),
        scratch_shapes=[pltpu.VMEM((n_planes, tm, LANES), F32)],
        compiler_params=pltpu.CompilerParams(
            dimension_semantics=("parallel",),
            vmem_limit_bytes=_vmem_limit(3 * tm * k * 2, k * d * 2, 4 * tm * d * 4, 2 * tm * d * 4,
                                         n_planes * tm * LANES * 4)),
        name="dil_outproj_ln",
    )(*os, *lses, w, x, g, b)


def kernel(x, w_qkv_sb, w_o_sb, w_qkv_dil, w_o_dil, ln_mix_g, ln_mix_b, w_gate_up, w_down, ln_ffn_g, ln_ffn_b):
    batch, seq, d_model = x.shape
    depth = ln_mix_g.shape[0]
    alpha = (2.0 * depth) ** 0.25
    t = batch * seq
    sb_heads = w_o_sb.shape[1] // HEAD_DIM
    assert all(window // dilation == DIL_SUB for window, dilation in DIL_CONFIGS)

    w_qkv_sb = w_qkv_sb.astype(BF16)
    later_weights = (w_o_sb, w_qkv_dil, w_o_dil, w_gate_up, w_down)
    cos2, sin2 = _rope_tables(seq)

    xf = x.reshape(t, d_model)
    for layer in range(depth):
        j = layer // 2
        g_mix, b_mix = ln_mix_g[layer][None, :], ln_mix_b[layer][None, :]
        if layer % 2 == 0:
            qkv = _proj_heads(xf, w_qkv_sb, j, LOG2_E / math.sqrt(HEAD_DIM))
            o, rounded = _sb_attention(qkv, batch, seq, sb_heads, later_weights if layer == 0 else ())
            if layer == 0:
                w_o_sb, w_qkv_dil, w_o_dil, w_gate_up, w_down = rounded
            xf = _outproj_ln(o, w_o_sb, j, xf, g_mix, b_mix, alpha)
        else:
            os, lses = [], []
            for group, (_, dilation) in enumerate(DIL_CONFIGS):
                q, k, v = _dil_proj(xf, w_qkv_dil, j, group, dilation, cos2, sin2, batch)
                o, lse = _dil_attention(q, k, v)
                os.append(o)
                lses.append(lse)
            xf = _dil_outproj_ln(os, lses, w_o_dil, j, xf, g_mix, b_mix, alpha)
        h = _ffn_up(xf, w_gate_up, layer)
        xf = _ffn_down_ln(h, w_down, layer, xf, ln_ffn_g[layer][None, :], ln_ffn_b[layer][None, :], alpha)
    return xf.reshape(batch, seq, d_model)
```

```python
import functools
import math

import jax
import jax.numpy as jnp
from jax import lax
from jax.experimental import pallas as pl
from jax.experimental.pallas import tpu as pltpu

HEAD_DIM = 128
DIL_CONFIGS = ((128, 1), (512, 4), (2048, 16))
DIL_HEADS_PER_GROUP = 6
ROPE_THETA = 10000.0
LN_EPS = 1e-5
LOG2_E = math.log2(math.e)

V7X_VMEM_BYTES = 64 * 1024 * 1024
V7X_MXU_DIM = 256
LANES = 128
BF16_SUBLANES = 16
SIGN_BIT = 0x80000000

PROJ_ROWS = 1024
PROJ_COLS = 1024
LN_ROWS = 512
LN_CHUNK_ROWS = 128
FFN_COLS = 512
FFN_DOWN_ROWS = 256
SB_Q_ROWS = 512
SB_KEY_BLOCK = V7X_MXU_DIM
SB_HEADS_PER_STEP = 4
DIL_PROJ_ROWS = 512
DEINTERLEAVE_STRIDE = 4
DIL_Q_ROWS = 1024
DIL_SUB = 128

BF16 = jnp.bfloat16
F32 = jnp.float32


def _vmem_limit(*nbytes):
    return int(min(V7X_VMEM_BYTES * 7 // 8, max(sum(nbytes) * 3 // 2, V7X_VMEM_BYTES // 4)))


def _layer_norm_rows(y, g, b):
    mu = jnp.mean(y, axis=-1, keepdims=True)
    d = y - mu
    var = jnp.mean(d * d, axis=-1, keepdims=True)
    return d * lax.rsqrt(var + LN_EPS) * g + b


def _proj_heads_kernel(x_ref, w_ref, o_ref, xb_ref, *, q_tiles, q_scale):
    j = pl.program_id(1)

    @pl.when(j == 0)
    def _():
        xb_ref[...] = x_ref[...].astype(BF16)

    acc = jnp.dot(xb_ref[...], w_ref[...], preferred_element_type=F32)
    acc = acc * jnp.where(j < q_tiles, q_scale, 1.0)
    for h in range(o_ref.shape[0]):
        o_ref[h] = acc[:, h * HEAD_DIM:(h + 1) * HEAD_DIM].astype(o_ref.dtype)


def _proj_heads(x, w, layer, q_scale):
    t, k = x.shape
    n = w.shape[2]
    tm, tn = min(PROJ_ROWS, t), min(PROJ_COLS, n // 3)
    hpt = tn // HEAD_DIM
    return pl.pallas_call(
        functools.partial(_proj_heads_kernel, q_tiles=n // 3 // tn, q_scale=q_scale),
        grid=(t // tm, n // tn),
        in_specs=[pl.BlockSpec((tm, k), lambda i, j: (i, 0)),
                  pl.BlockSpec((None, k, tn), lambda i, j: (layer, 0, j))],
        out_specs=pl.BlockSpec((hpt, tm, HEAD_DIM), lambda i, j: (j, i, 0)),
        out_shape=jax.ShapeDtypeStruct((n // HEAD_DIM, t, HEAD_DIM), BF16),
        scratch_shapes=[pltpu.VMEM((tm, k), BF16)],
        compiler_params=pltpu.CompilerParams(
            dimension_semantics=("parallel", "arbitrary"),
            vmem_limit_bytes=_vmem_limit(2 * tm * k * 4, tm * k * 2, 2 * k * tn * 2, 2 * tm * tn * 2, tm * tn * 4)),
        name="sb_qkv_proj",
    )(x, w)


def _sb_attn_kernel(q_ref, k_ref, v_ref, tri_ref, *rest):
    n_cast = (len(rest) - 3) // 2
    cast_in, o_ref, cast_out = rest[:n_cast], rest[n_cast], rest[n_cast + 1:2 * n_cast + 1]
    acc_ref, run_ref = rest[2 * n_cast + 1:]
    for src, dst in zip(cast_in, cast_out):
        dst[...] = src[...].astype(dst.dtype)
    n_heads, tq, _ = q_ref.shape
    tk = SB_KEY_BLOCK
    diag_blocks = tq // tk
    i = pl.program_id(2)
    acc_ref[...] = jnp.zeros_like(acc_ref)
    run_ref[...] = jnp.zeros_like(run_ref)

    def visit(k0, row0, n_blocks, masked):
        heads = range(n_heads)
        zs, costs = [], []
        for h in heads:
            q = q_ref[h, row0:, :]
            kblk = k_ref[h, pl.ds(k0, n_blocks * tk), :]
            zs.append(lax.dot_general(q, kblk, (((1,), (1,)), ((), ())), preferred_element_type=F32))
        if masked:
            qpos = i * tq + row0 + lax.broadcasted_iota(jnp.int32, zs[0].shape, 0)
            kpos = k0 + lax.broadcasted_iota(jnp.int32, zs[0].shape, 1)
            causal = kpos < qpos
        for z in zs:
            neg_abs = pltpu.bitcast(pltpu.bitcast(z, jnp.uint32) | jnp.uint32(SIGN_BIT), F32)
            cost = jnp.maximum(z, 0.0) + jnp.log(1.0 + jnp.exp2(neg_abs)) * LOG2_E
            if masked:
                cost = jnp.where(causal, cost, 0.0)
            costs.append(cost.astype(BF16))
        runs = [run_ref[h, row0:, :] for h in heads]
        pieces = [[None] * n_blocks for _ in heads]
        for s in reversed(range(n_blocks)):
            cols = slice(s * tk, (s + 1) * tk)
            for h in heads:
                incl = jnp.dot(costs[h][:, cols], tri_ref[...], preferred_element_type=F32)
                pieces[h][s] = jnp.exp2(zs[h][:, cols] - incl - runs[h])
                runs[h] = runs[h] + incl[:, 0:1]
        for h in heads:
            w = pieces[h][0] if n_blocks == 1 else jnp.concatenate(pieces[h], axis=1)
            if masked:
                w = jnp.where(causal, w, 0.0)
            vblk = v_ref[h, pl.ds(k0, n_blocks * tk), :]
            acc_ref[h, row0:, :] += jnp.dot(w.astype(BF16), vblk, preferred_element_type=F32)
            run_ref[h, row0:, :] = runs[h]

    diag0 = pl.multiple_of(i * tq, tq)
    for d in reversed(range(diag_blocks)):
        visit(diag0 + d * tk, d * tk, 1, True)

    def body(n, carry):
        visit(pl.multiple_of((i - 1 - n) * tq, tq), 0, diag_blocks, False)
        return carry

    lax.fori_loop(0, i, body, 0)
    o_ref[...] = acc_ref[...].astype(o_ref.dtype)


def _cast_slab_rows(rows, n_steps):
    slab = -(-rows // n_steps)
    slab += -slab % BF16_SUBLANES
    while rows % slab:
        slab += BF16_SUBLANES
    return slab


def _sb_attention(qkv, batch, seq, n_heads, weights=()):
    tq, tk = min(SB_Q_ROWS, seq), SB_KEY_BLOCK
    nq = seq // tq
    hp = SB_HEADS_PER_STEP
    ng = n_heads // hp
    n_steps = ng * batch * nq
    tri = (lax.broadcasted_iota(jnp.int32, (tk, tk), 0) >= lax.broadcasted_iota(jnp.int32, (tk, tk), 1)).astype(BF16)
    flat = [w.reshape(-1, w.shape[-1]) for w, _ in weights]
    out_shapes = [(w.shape[0] - skip,) + w.shape[1:] for w, skip in weights]
    in_slabs, out_slabs, slab_bytes = [], [], 0
    for (w, skip), out_shape in zip(weights, out_shapes):
        rows, cols = out_shape[0] * out_shape[1], out_shape[2]
        slab = _cast_slab_rows(rows, n_steps)
        assert w.shape[1] % slab == 0
        last, first = rows // slab - 1, skip * w.shape[1] // slab
        step = lambda g, b, i, last=last: jnp.minimum((g * batch + b) * nq + i, last)
        in_slabs.append(pl.BlockSpec((slab, cols), lambda g, b, i, step=step, first=first: (first + step(g, b, i), 0)))
        out_slabs.append(pl.BlockSpec((slab, cols), lambda g, b, i, step=step: (step(g, b, i), 0)))
        slab_bytes += 2 * slab * cols * (4 + 2)
    outs = pl.pallas_call(
        _sb_attn_kernel,
        grid=(ng, batch, nq),
        in_specs=[pl.BlockSpec((hp, tq, HEAD_DIM), lambda g, b, i: (g, b * nq + i, 0)),
                  pl.BlockSpec((hp, seq, HEAD_DIM), lambda g, b, i: (ng + g, b, 0)),
                  pl.BlockSpec((hp, seq, HEAD_DIM), lambda g, b, i: (2 * ng + g, b, 0)),
                  pl.BlockSpec((tk, tk), lambda g, b, i: (0, 0))] + in_slabs,
        out_specs=[pl.BlockSpec((hp, tq, HEAD_DIM), lambda g, b, i: (g, b * nq + i, 0))] + out_slabs,
        out_shape=[jax.ShapeDtypeStruct((n_heads, batch * seq, HEAD_DIM), BF16)] +
                  [jax.ShapeDtypeStruct((s[0] * s[1], s[2]), BF16) for s in out_shapes],
        scratch_shapes=[pltpu.VMEM((hp, tq, HEAD_DIM), F32), pltpu.VMEM((hp, tq, 1), F32)],
        compiler_params=pltpu.CompilerParams(
            dimension_semantics=("arbitrary", "arbitrary", "arbitrary"),
            vmem_limit_bytes=_vmem_limit(4 * hp * seq * HEAD_DIM * 2, 8 * hp * tq * tq * 4, slab_bytes)),
        name="sb_attention",
    )(qkv, qkv, qkv, tri, *flat)
    return outs[0], [o.reshape(s) for o, s in zip(outs[1:], out_shapes)]


def _project_residual_ln(lhs, w_ref, x_ref, g_ref, b_ref, y_ref, alpha):
    rows = min(LN_CHUNK_ROWS, lhs.shape[0])
    for c in range(lhs.shape[0] // rows):
        sl = slice(c * rows, (c + 1) * rows)
        mixed = jnp.dot(lhs[sl], w_ref[...], preferred_element_type=F32)
        y_ref[sl, :] = _layer_norm_rows(alpha * x_ref[sl, :] + mixed, g_ref[...], b_ref[...])


def _outproj_ln_kernel(o_ref, w_ref, x_ref, g_ref, b_ref, y_ref, *, alpha):
    lhs = jnp.concatenate([o_ref[h] for h in range(o_ref.shape[0])], axis=1)
    _project_residual_ln(lhs, w_ref, x_ref, g_ref, b_ref, y_ref, alpha)


def _outproj_ln(o, w, layer, x, g, b, alpha):
    n_heads, t, _ = o.shape
    _, k, d = w.shape
    tm = min(LN_ROWS, t)
    return pl.pallas_call(
        functools.partial(_outproj_ln_kernel, alpha=alpha),
        grid=(t // tm,),
        in_specs=[pl.BlockSpec((n_heads, tm, HEAD_DIM), lambda i: (0, i, 0)),
                  pl.BlockSpec((None, k, d), lambda i: (layer, 0, 0), pipeline_mode=pl.Buffered(1)),
                  pl.BlockSpec((tm, d), lambda i: (i, 0)),
                  pl.BlockSpec((1, d), lambda i: (0, 0)),
                  pl.BlockSpec((1, d), lambda i: (0, 0))],
        out_specs=pl.BlockSpec((tm, d), lambda i: (i, 0)),
        out_shape=jax.ShapeDtypeStruct((t, d), F32),
        compiler_params=pltpu.CompilerParams(
            dimension_semantics=("parallel",),
            vmem_limit_bytes=_vmem_limit(2 * tm * k * 2, k * d * 2, 4 * tm * d * 4, 2 * tm * d * 4)),
        name="sb_outproj_ln",
    )(o, w, x, g, b)


def _ffn_up_kernel(x_ref, wg_ref, wu_ref, h_ref, xb_ref):
    @pl.when(pl.program_id(1) == 0)
    def _():
        xb_ref[...] = x_ref[...].astype(BF16)

    xb = xb_ref[...]
    gate = jnp.dot(xb, wg_ref[...], preferred_element_type=F32)
    up = jnp.dot(xb, wu_ref[...], preferred_element_type=F32)
    h_ref[...] = (jax.nn.silu(gate) * up).astype(h_ref.dtype)


def _ffn_up(x, w_gate_up, layer):
    t, d = x.shape
    f = w_gate_up.shape[2] // 2
    tm, tn = min(PROJ_ROWS, t), min(FFN_COLS, f)
    nj = f // tn
    return pl.pallas_call(
        _ffn_up_kernel,
        grid=(t // tm, nj),
        in_specs=[pl.BlockSpec((tm, d), lambda i, j: (i, 0)),
                  pl.BlockSpec((None, d, tn), lambda i, j: (layer, 0, j)),
                  pl.BlockSpec((None, d, tn), lambda i, j: (layer, 0, nj + j))],
        out_specs=pl.BlockSpec((tm, tn), lambda i, j: (i, j)),
        out_shape=jax.ShapeDtypeStruct((t, f), BF16),
        scratch_shapes=[pltpu.VMEM((tm, d), BF16)],
        compiler_params=pltpu.CompilerParams(
            dimension_semantics=("parallel", "arbitrary"),
            vmem_limit_bytes=_vmem_limit(2 * tm * d * 4, tm * d * 2, 4 * d * tn * 2, 2 * tm * tn * 2, 3 * tm * tn * 4)),
        name="ffn_up",
    )(x, w_gate_up, w_gate_up)


def _ffn_down_ln_kernel(h_ref, w_ref, x_ref, g_ref, b_ref, y_ref, *, alpha):
    _project_residual_ln(h_ref[...], w_ref, x_ref, g_ref, b_ref, y_ref, alpha)


def _ffn_down_ln(h, w, layer, x, g, b, alpha):
    t, f = h.shape
    d = w.shape[2]
    tm = min(FFN_DOWN_ROWS, t)
    return pl.pallas_call(
        functools.partial(_ffn_down_ln_kernel, alpha=alpha),
        grid=(t // tm,),
        in_specs=[pl.BlockSpec((tm, f), lambda i: (i, 0)),
                  pl.BlockSpec((None, f, d), lambda i: (layer, 0, 0), pipeline_mode=pl.Buffered(1)),
                  pl.BlockSpec((tm, d), lambda i: (i, 0)),
                  pl.BlockSpec((1, d), lambda i: (0, 0)),
                  pl.BlockSpec((1, d), lambda i: (0, 0))],
        out_specs=pl.BlockSpec((tm, d), lambda i: (i, 0)),
        out_shape=jax.ShapeDtypeStruct((t, d), F32),
        compiler_params=pltpu.CompilerParams(
            dimension_semantics=("parallel",),
            vmem_limit_bytes=_vmem_limit(f * d * 2, 2 * tm * f * 2, 4 * tm * d * 4, 2 * tm * d * 4)),
        name="ffn_down_ln",
    )(h, w, x, g, b)


def _rope_tables(seq):
    pos = jnp.arange(seq, dtype=F32)
    inv_freq = ROPE_THETA ** (-jnp.arange(0, HEAD_DIM, 2, dtype=F32) / HEAD_DIM)
    ang = pos[:, None] * inv_freq[None, :]
    cos, sin = jnp.cos(ang), jnp.sin(ang)
    return jnp.concatenate([cos, cos], axis=1), jnp.concatenate([-sin, sin], axis=1)


def _dil_proj_kernel(x_ref, wq_ref, wk_ref, wv_ref, cosq_ref, sinq_ref, cos_ref, sin_ref, q_ref, k_ref, v_ref,
                     xp_ref, *planes):
    _, r, tu, _ = q_ref.shape
    if r == 1:
        xp_ref[...] = x_ref[...].astype(BF16)
    else:
        src, dst = planes
        n_planes = src.shape[0]
        for j in range(n_planes):
            src[j] = x_ref[:, j * LANES:(j + 1) * LANES]
        blocks, mult, ways = [(0, src.shape[1], 0)], 1, r
        while ways > DEINTERLEAVE_STRIDE:
            split = []
            for off, n, base in blocks:
                part = n // DEINTERLEAVE_STRIDE
                for c in range(DEINTERLEAVE_STRIDE):
                    for j in range(n_planes):
                        dst[j, off + c * part:off + (c + 1) * part, :] = (
                            src[j, pl.ds(off + c, part, stride=DEINTERLEAVE_STRIDE), :])
                    split.append((off + c * part, part, base + c * mult))
            blocks, mult, ways = split, mult * DEINTERLEAVE_STRIDE, ways // DEINTERLEAVE_STRIDE
            src, dst = dst, src
        for off, n, base in blocks:
            for m in range(ways):
                c = base + m * mult
                for j in range(n_planes):
                    rows = src[j, pl.ds(off + m, tu, stride=ways), :]
                    xp_ref[c * tu:(c + 1) * tu, j * LANES:(j + 1) * LANES] = rows.astype(BF16)
    xb = xp_ref[...]
    for w_ref, out_ref, rope in ((wq_ref, q_ref, (cosq_ref, sinq_ref)), (wk_ref, k_ref, (cos_ref, sin_ref)),
                                 (wv_ref, v_ref, None)):
        acc = jnp.dot(xb, w_ref[...], preferred_element_type=F32)
        for h in range(out_ref.shape[0]):
            a = acc[:, h * HEAD_DIM:(h + 1) * HEAD_DIM]
            if rope:
                a = a * rope[0][...] + pltpu.roll(a, HEAD_DIM // 2, 1) * rope[1][...]
            a = a.astype(out_ref.dtype)
            for c in range(r):
                out_ref[h, c] = a[c * tu:(c + 1) * tu, :]


def _dil_proj(x, w, layer, group, dilation, cos2, sin2, batch, q_scale):
    t, d = x.shape
    s = t // batch
    r = dilation
    length = s // r
    tm = min(DIL_PROJ_ROWS, s)
    tu = tm // r
    nt = s // tm
    planes = [] if r == 1 else [pltpu.VMEM((d // LANES, tm, LANES), F32)] * 2
    n_groups = len(DIL_CONFIGS)
    gw = DIL_HEADS_PER_GROUP * HEAD_DIM
    cos_p = cos2.reshape(nt, tu, r, HEAD_DIM).transpose(0, 2, 1, 3).reshape(s, HEAD_DIM)
    sin_p = sin2.reshape(nt, tu, r, HEAD_DIM).transpose(0, 2, 1, 3).reshape(s, HEAD_DIM)
    w_spec = lambda kind: pl.BlockSpec((None, d, gw), lambda i: (layer, 0, kind * n_groups + group),
                                       pipeline_mode=pl.Buffered(1))
    tab_spec = pl.BlockSpec((tm, HEAD_DIM), lambda i: (i % nt, 0))
    out_spec = pl.BlockSpec((DIL_HEADS_PER_GROUP, None, r, tu, HEAD_DIM), lambda i: (0, i // nt, 0, i % nt, 0))
    out_sds = jax.ShapeDtypeStruct((DIL_HEADS_PER_GROUP, batch, r, length, HEAD_DIM), BF16)
    return pl.pallas_call(
        _dil_proj_kernel,
        grid=(t // tm,),
        in_specs=[pl.BlockSpec((tm, d), lambda i: (i, 0)),
                  w_spec(0), w_spec(1), w_spec(2), tab_spec, tab_spec, tab_spec, tab_spec],
        out_specs=[out_spec, out_spec, out_spec],
        out_shape=[out_sds, out_sds, out_sds],
        scratch_shapes=[pltpu.VMEM((tm, d), BF16)] + planes,
        compiler_params=pltpu.CompilerParams(
            dimension_semantics=("parallel",),
            vmem_limit_bytes=_vmem_limit(4 * tm * d * 4, tm * d * 2, 3 * d * gw * 2, 6 * tm * gw * 2, 2 * tm * gw * 4)),
        name=f"dil_qkv_proj_r{r}",
    )(x, w, w, w, cos_p * q_scale, sin_p * q_scale, cos_p, sin_p)


def _dil_attn_kernel(q_ref, k_ref, v_ref, kp_ref, vp_ref, o_ref, lse_ref):
    n_heads, n_res, tu, _ = q_ref.shape
    sub = DIL_SUB
    iu = pl.program_id(2)
    qi = lax.broadcasted_iota(jnp.int32, (sub, 2 * sub), 0)
    kj = lax.broadcasted_iota(jnp.int32, (sub, 2 * sub), 1)
    in_window = (kj >= qi) & (kj <= qi + sub)
    first_valid = in_window & (kj >= jnp.where(iu > 0, 0, sub))
    lane = lax.broadcasted_iota(jnp.int32, (sub, LANES), 1)
    for c in range(n_res):
        for s in range(tu // sub):
            rows = slice(s * sub, (s + 1) * sub)
            valid = first_valid if s == 0 else in_window
            lse_tile = jnp.zeros((sub, LANES), F32)
            for h in range(n_heads):
                q = q_ref[h, c, rows, :]
                if s == 0:
                    k_prev, v_prev = kp_ref[h, c], vp_ref[h, c]
                else:
                    k_prev, v_prev = k_ref[h, c, (s - 1) * sub:s * sub, :], v_ref[h, c, (s - 1) * sub:s * sub, :]
                keys = jnp.concatenate([k_prev, k_ref[h, c, rows, :]], axis=0)
                vals = jnp.concatenate([v_prev, v_ref[h, c, rows, :]], axis=0)
                sc = lax.dot_general(q, keys, (((1,), (1,)), ((), ())), preferred_element_type=F32)
                sc = jnp.where(valid, sc, -jnp.inf)
                m = jnp.max(sc, axis=1, keepdims=True)
                p = jnp.exp2(sc - m)
                den = jnp.sum(p, axis=1, keepdims=True)
                o = jnp.dot(p.astype(BF16), vals, preferred_element_type=F32) / den
                o_ref[c, rows, h * HEAD_DIM:(h + 1) * HEAD_DIM] = o.astype(o_ref.dtype)
                lse_tile = jnp.where(lane == h, m + jnp.log(den) * LOG2_E, lse_tile)
            lse_ref[c, rows, :] = lse_tile


def _dil_attention(q, k, v):
    n_heads, b, r, length, _ = q.shape
    tu = min(DIL_Q_ROWS, length)
    rc = min(r, DIL_Q_ROWS // tu)
    sub = DIL_SUB
    gw = n_heads * HEAD_DIM
    cur = pl.BlockSpec((n_heads, None, rc, tu, HEAD_DIM), lambda bi, c, iu: (0, bi, c, iu, 0))
    prev = pl.BlockSpec((n_heads, None, rc, sub, HEAD_DIM),
                        lambda bi, c, iu: (0, bi, c, jnp.maximum(iu * (tu // sub) - 1, 0), 0))
    return pl.pallas_call(
        _dil_attn_kernel,
        grid=(b, r // rc, length // tu),
        in_specs=[cur, cur, cur, prev, prev],
        out_specs=[pl.BlockSpec((None, rc, tu, gw), lambda bi, c, iu: (bi, c, iu, 0)),
                   pl.BlockSpec((None, rc, tu, LANES), lambda bi, c, iu: (bi, c, iu, 0))],
        out_shape=[jax.ShapeDtypeStruct((b, r, length, gw), BF16),
                   jax.ShapeDtypeStruct((b, r, length, LANES), F32)],
        compiler_params=pltpu.CompilerParams(
            dimension_semantics=("parallel", "parallel", "parallel"),
            vmem_limit_bytes=_vmem_limit(6 * rc * tu * gw * 2, 4 * rc * sub * gw * 2, 2 * rc * tu * gw * 2,
                                         2 * rc * tu * LANES * 4)),
        name=f"dil_attention_r{r}",
    )(q, k, v, k, v)


def _dil_outproj_ln_kernel(o0_ref, o1_ref, o2_ref, l0_ref, l1_ref, l2_ref, w_ref, x_ref, g_ref, b_ref, y_ref,
                           planes_ref, *, alpha):
    o_refs, l_refs = (o0_ref, o1_ref, o2_ref), (l0_ref, l1_ref, l2_ref)
    n_o = len(o_refs) * DIL_HEADS_PER_GROUP

    def natural(ref, cols, plane):
        r, tu, _ = ref.shape
        if r == 1:
            return ref[0, :, cols].astype(F32)
        if r <= DEINTERLEAVE_STRIDE:
            for c in range(r):
                planes_ref[plane, pl.ds(c, tu, stride=r), :] = ref[c, :, cols].astype(F32)
            return planes_ref[plane]
        narrow, wide = DEINTERLEAVE_STRIDE, r // DEINTERLEAVE_STRIDE
        stage, part = planes_ref.shape[0] - 1, tu * wide
        for c in range(r):
            lo, hi = c % narrow, c // narrow
            planes_ref[stage, pl.ds(lo * part + hi, tu, stride=wide), :] = ref[c, :, cols].astype(F32)
        for lo in range(narrow):
            planes_ref[plane, pl.ds(lo, part, stride=narrow), :] = planes_ref[stage, lo * part:(lo + 1) * part, :]
        return planes_ref[plane]

    lses = [natural(l_ref, slice(0, LANES), n_o + g) for g, l_ref in enumerate(l_refs)]
    mx = jnp.maximum(jnp.maximum(lses[0], lses[1]), lses[2])
    es = [jnp.exp2(l - mx) for l in lses]
    den = es[0] + es[1] + es[2]
    pieces = []
    for g, (o_ref, e) in enumerate(zip(o_refs, es)):
        share = e / den
        for h in range(DIL_HEADS_PER_GROUP):
            o = natural(o_ref, slice(h * HEAD_DIM, (h + 1) * HEAD_DIM), g * DIL_HEADS_PER_GROUP + h)
            pieces.append((o * share[:, h:h + 1]).astype(BF16))
    _project_residual_ln(jnp.concatenate(pieces, axis=1), w_ref, x_ref, g_ref, b_ref, y_ref, alpha)


def _dil_outproj_ln(os, lses, w, layer, x, g, b, alpha):
    t, d = x.shape
    _, k, _ = w.shape
    batch = os[0].shape[0]
    tm = min(LN_ROWS, t // batch)
    nt = t // batch // tm

    def res_spec(a):
        _, r, _, width = a.shape
        return pl.BlockSpec((None, r, tm // r, width), lambda i: (i // nt, 0, i % nt, 0))

    n_planes = len(os) * DIL_HEADS_PER_GROUP + len(lses) + 1
    return pl.pallas_call(
        functools.partial(_dil_outproj_ln_kernel, alpha=alpha),
        grid=(t // tm,),
        in_specs=[res_spec(a) for a in (*os, *lses)] + [
                  pl.BlockSpec((None, k, d), lambda i: (layer, 0, 0), pipeline_mode=pl.Buffered(1)),
                  pl.BlockSpec((tm, d), lambda i: (i, 0)),
                  pl.BlockSpec((1, d), lambda i: (0, 0)),
                  pl.BlockSpec((1, d), lambda i: (0, 0))],
        out_specs=pl.BlockSpec((tm, d), lambda i: (i, 0)),
        out_shape=jax.ShapeDtypeStruct((t, d), F32),
        scratch_shapes=[pltpu.VMEM((n_planes, tm, LANES), F32)],
        compiler_params=pltpu.CompilerParams(
            dimension_semantics=("parallel",),
            vmem_limit_bytes=_vmem_limit(3 * tm * k * 2, k * d * 2, 4 * tm * d * 4, 2 * tm * d * 4,
                                         n_planes * tm * LANES * 4)),
        name="dil_outproj_ln",
    )(*os, *lses, w, x, g, b)


def kernel(x, w_qkv_sb, w_o_sb, w_qkv_dil, w_o_dil, ln_mix_g, ln_mix_b, w_gate_up, w_down, ln_ffn_g, ln_ffn_b):
    batch, seq, d_model = x.shape
    depth = ln_mix_g.shape[0]
    alpha = (2.0 * depth) ** 0.25
    t = batch * seq
    sb_heads = w_o_sb.shape[1] // HEAD_DIM
    assert all(window // dilation == DIL_SUB for window, dilation in DIL_CONFIGS)

    w_qkv_first = w_qkv_sb[:1].astype(BF16)
    later_weights = ((w_qkv_sb, 1), (w_o_sb, 0), (w_qkv_dil, 0), (w_o_dil, 0), (w_gate_up, 0), (w_down, 0))
    cos2, sin2 = _rope_tables(seq)
    q_scale = LOG2_E / math.sqrt(HEAD_DIM)

    xf = x.reshape(t, d_model)
    for layer in range(depth):
        j = layer // 2
        g_mix, b_mix = ln_mix_g[layer][None, :], ln_mix_b[layer][None, :]
        if layer % 2 == 0:
            qkv = _proj_heads(xf, w_qkv_first, 0, q_scale) if layer == 0 else _proj_heads(xf, w_qkv_rest, j - 1, q_scale)
            o, rounded = _sb_attention(qkv, batch, seq, sb_heads, later_weights if layer == 0 else ())
            if layer == 0:
                w_qkv_rest, w_o_sb, w_qkv_dil, w_o_dil, w_gate_up, w_down = rounded
            xf = _outproj_ln(o, w_o_sb, j, xf, g_mix, b_mix, alpha)
        else:
            os, lses = [], []
            for group, (_, dilation) in enumerate(DIL_CONFIGS):
                q, k, v = _dil_proj(xf, w_qkv_dil, j, group, dilation, cos2, sin2, batch, q_scale)
                o, lse = _dil_attention(q, k, v)
                os.append(o)
                lses.append(lse)
            xf = _dil_outproj_ln(os, lses, w_o_dil, j, xf, g_mix, b_mix, alpha)
        h = _ffn_up(xf, w_gate_up, layer)
        xf = _ffn_down_ln(h, w_down, layer, xf, ln_ffn_g[layer][None, :], ln_ffn_b[layer][None, :], alpha)
    return xf.reshape(batch, seq, d_model)
```

```python
import functools
import math

import jax
import jax.numpy as jnp
from jax import lax
from jax.experimental import pallas as pl
from jax.experimental.pallas import tpu as pltpu

HEAD_DIM = 128
DIL_CONFIGS = ((128, 1), (512, 4), (2048, 16))
DIL_HEADS_PER_GROUP = 6
ROPE_THETA = 10000.0
LN_EPS = 1e-5
LOG2_E = math.log2(math.e)

V7X_VMEM_BYTES = 64 * 1024 * 1024
V7X_MXU_DIM = 256
LANES = 128
BF16_SUBLANES = 16
SB_MAX_LOGIT2 = 126.0
PROJ_ROWS = 1024
PROJ_COLS = 1024
LN_ROWS = 512
LN_CHUNK_ROWS = 128
FFN_COLS = 512
FFN_DOWN_ROWS = 256
SB_Q_ROWS = 512
SB_KEY_BLOCK = V7X_MXU_DIM
SB_HEADS_PER_STEP = 4
DIL_PROJ_ROWS = 512
DEINTERLEAVE_STRIDE = 4
DIL_Q_ROWS = 1024
DIL_SUB = 128

BF16 = jnp.bfloat16
F32 = jnp.float32


def _vmem_limit(*nbytes):
    return int(min(V7X_VMEM_BYTES * 7 // 8, max(sum(nbytes) * 3 // 2, V7X_VMEM_BYTES // 4)))


def _layer_norm_rows(y, g, b):
    mu = jnp.mean(y, axis=-1, keepdims=True)
    d = y - mu
    var = jnp.mean(d * d, axis=-1, keepdims=True)
    return d * lax.rsqrt(var + LN_EPS) * g + b


def _proj_heads_kernel(x_ref, w_ref, o_ref, xb_ref, *, q_tiles, q_scale):
    j = pl.program_id(1)

    @pl.when(j == 0)
    def _():
        xb_ref[...] = x_ref[...].astype(BF16)

    acc = jnp.dot(xb_ref[...], w_ref[...], preferred_element_type=F32)
    acc = acc * jnp.where(j < q_tiles, q_scale, 1.0)
    for h in range(o_ref.shape[0]):
        o_ref[h] = acc[:, h * HEAD_DIM:(h + 1) * HEAD_DIM].astype(o_ref.dtype)


def _proj_heads(x, w, layer, q_scale):
    t, k = x.shape
    n = w.shape[2]
    tm, tn = min(PROJ_ROWS, t), min(PROJ_COLS, n // 3)
    hpt = tn // HEAD_DIM
    return pl.pallas_call(
        functools.partial(_proj_heads_kernel, q_tiles=n // 3 // tn, q_scale=q_scale),
        grid=(t // tm, n // tn),
        in_specs=[pl.BlockSpec((tm, k), lambda i, j: (i, 0)),
                  pl.BlockSpec((None, k, tn), lambda i, j: (layer, 0, j))],
        out_specs=pl.BlockSpec((hpt, tm, HEAD_DIM), lambda i, j: (j, i, 0)),
        out_shape=jax.ShapeDtypeStruct((n // HEAD_DIM, t, HEAD_DIM), BF16),
        scratch_shapes=[pltpu.VMEM((tm, k), BF16)],
        compiler_params=pltpu.CompilerParams(
            dimension_semantics=("parallel", "arbitrary"),
            vmem_limit_bytes=_vmem_limit(2 * tm * k * 4, tm * k * 2, 2 * k * tn * 2, 2 * tm * tn * 2, tm * tn * 4)),
        name="sb_qkv_proj",
    )(x, w)


def _sb_attn_kernel(q_ref, k_ref, v_ref, tri_ref, *rest):
    n_cast = (len(rest) - 3) // 2
    cast_in, o_ref, cast_out = rest[:n_cast], rest[n_cast], rest[n_cast + 1:2 * n_cast + 1]
    acc_ref, run_ref = rest[2 * n_cast + 1:]
    for src, dst in zip(cast_in, cast_out):
        dst[...] = src[...].astype(dst.dtype)
    n_heads, tq, _ = q_ref.shape
    tk = SB_KEY_BLOCK
    diag_blocks = tq // tk
    i = pl.program_id(2)
    acc_ref[...] = jnp.zeros_like(acc_ref)
    run_ref[...] = jnp.zeros_like(run_ref)

    def visit(k0, row0, n_blocks, masked):
        heads = range(n_heads)
        zs, costs = [], []
        for h in heads:
            q = q_ref[h, row0:, :]
            kblk = k_ref[h, pl.ds(k0, n_blocks * tk), :]
            zs.append(lax.dot_general(q, kblk, (((1,), (1,)), ((), ())), preferred_element_type=F32))
        if masked:
            qpos = i * tq + row0 + lax.broadcasted_iota(jnp.int32, zs[0].shape, 0)
            kpos = k0 + lax.broadcasted_iota(jnp.int32, zs[0].shape, 1)
            causal = kpos < qpos
        for z in zs:
            cost = jnp.maximum(z, jnp.log(1.0 + jnp.exp2(jnp.minimum(z, SB_MAX_LOGIT2))) * LOG2_E)
            if masked:
                cost = jnp.where(causal, cost, 0.0)
            costs.append(cost.astype(BF16))
        runs = [run_ref[h, row0:, :] for h in heads]
        pieces = [[None] * n_blocks for _ in heads]
        for s in reversed(range(n_blocks)):
            cols = slice(s * tk, (s + 1) * tk)
            for h in heads:
                incl = jnp.dot(costs[h][:, cols], tri_ref[...], preferred_element_type=F32)
                pieces[h][s] = jnp.exp2(zs[h][:, cols] - incl - runs[h])
                runs[h] = runs[h] + incl[:, 0:1]
        for h in heads:
            w = pieces[h][0] if n_blocks == 1 else jnp.concatenate(pieces[h], axis=1)
            if masked:
                w = jnp.where(causal, w, 0.0)
            vblk = v_ref[h, pl.ds(k0, n_blocks * tk), :]
            acc_ref[h, row0:, :] += jnp.dot(w.astype(BF16), vblk, preferred_element_type=F32)
            run_ref[h, row0:, :] = runs[h]

    diag0 = pl.multiple_of(i * tq, tq)
    for d in reversed(range(diag_blocks)):
        visit(diag0 + d * tk, d * tk, 1, True)

    def body(n, carry):
        visit(pl.multiple_of((i - 1 - n) * tq, tq), 0, diag_blocks, False)
        return carry

    lax.fori_loop(0, i, body, 0)
    o_ref[...] = acc_ref[...].astype(o_ref.dtype)


def _cast_slab_rows(rows, n_steps):
    slab = -(-rows // n_steps)
    slab += -slab % BF16_SUBLANES
    while rows % slab:
        slab += BF16_SUBLANES
    return slab


def _sb_attention(qkv, batch, seq, n_heads, weights=()):
    tq, tk = min(SB_Q_ROWS, seq), SB_KEY_BLOCK
    nq = seq // tq
    hp = SB_HEADS_PER_STEP
    ng = n_heads // hp
    n_steps = ng * batch * nq
    tri = (lax.broadcasted_iota(jnp.int32, (tk, tk), 0) >= lax.broadcasted_iota(jnp.int32, (tk, tk), 1)).astype(BF16)
    flat = [w.reshape(-1, w.shape[-1]) for w, _ in weights]
    out_shapes = [(w.shape[0] - skip,) + w.shape[1:] for w, skip in weights]
    in_slabs, out_slabs, slab_bytes = [], [], 0
    for (w, skip), out_shape in zip(weights, out_shapes):
        rows, cols = out_shape[0] * out_shape[1], out_shape[2]
        slab = _cast_slab_rows(rows, n_steps)
        assert w.shape[1] % slab == 0
        last, first = rows // slab - 1, skip * w.shape[1] // slab
        step = lambda g, b, i, last=last: jnp.minimum((g * batch + b) * nq + i, last)
        in_slabs.append(pl.BlockSpec((slab, cols), lambda g, b, i, step=step, first=first: (first + step(g, b, i), 0)))
        out_slabs.append(pl.BlockSpec((slab, cols), lambda g, b, i, step=step: (step(g, b, i), 0)))
        slab_bytes += 2 * slab * cols * (4 + 2)
    outs = pl.pallas_call(
        _sb_attn_kernel,
        grid=(ng, batch, nq),
        in_specs=[pl.BlockSpec((hp, tq, HEAD_DIM), lambda g, b, i: (g, b * nq + i, 0)),
                  pl.BlockSpec((hp, seq, HEAD_DIM), lambda g, b, i: (ng + g, b, 0)),
                  pl.BlockSpec((hp, seq, HEAD_DIM), lambda g, b, i: (2 * ng + g, b, 0)),
                  pl.BlockSpec((tk, tk), lambda g, b, i: (0, 0))] + in_slabs,
        out_specs=[pl.BlockSpec((hp, tq, HEAD_DIM), lambda g, b, i: (g, b * nq + i, 0))] + out_slabs,
        out_shape=[jax.ShapeDtypeStruct((n_heads, batch * seq, HEAD_DIM), BF16)] +
                  [jax.ShapeDtypeStruct((s[0] * s[1], s[2]), BF16) for s in out_shapes],
        scratch_shapes=[pltpu.VMEM((hp, tq, HEAD_DIM), F32), pltpu.VMEM((hp, tq, 1), F32)],
        compiler_params=pltpu.CompilerParams(
            dimension_semantics=("arbitrary", "arbitrary", "arbitrary"),
            vmem_limit_bytes=_vmem_limit(4 * hp * seq * HEAD_DIM * 2, 8 * hp * tq * tq * 4, slab_bytes)),
        name="sb_attention",
    )(qkv, qkv, qkv, tri, *flat)
    return outs[0], [o.reshape(s) for o, s in zip(outs[1:], out_shapes)]


def _project_residual_ln(lhs, w_ref, x_ref, g_ref, b_ref, y_ref, alpha):
    rows = min(LN_CHUNK_ROWS, lhs.shape[0])
    for c in range(lhs.shape[0] // rows):
        sl = slice(c * rows, (c + 1) * rows)
        mixed = jnp.dot(lhs[sl], w_ref[...], preferred_element_type=F32)
        y_ref[sl, :] = _layer_norm_rows(alpha * x_ref[sl, :] + mixed, g_ref[...], b_ref[...])


def _outproj_ln_kernel(o_ref, w_ref, x_ref, g_ref, b_ref, y_ref, *, alpha):
    lhs = jnp.concatenate([o_ref[h] for h in range(o_ref.shape[0])], axis=1)
    _project_residual_ln(lhs, w_ref, x_ref, g_ref, b_ref, y_ref, alpha)


def _outproj_ln(o, w, layer, x, g, b, alpha):
    n_heads, t, _ = o.shape
    _, k, d = w.shape
    tm = min(LN_ROWS, t)
    return pl.pallas_call(
        functools.partial(_outproj_ln_kernel, alpha=alpha),
        grid=(t // tm,),
        in_specs=[pl.BlockSpec((n_heads, tm, HEAD_DIM), lambda i: (0, i, 0)),
                  pl.BlockSpec((None, k, d), lambda i: (layer, 0, 0), pipeline_mode=pl.Buffered(1)),
                  pl.BlockSpec((tm, d), lambda i: (i, 0)),
                  pl.BlockSpec((1, d), lambda i: (0, 0)),
                  pl.BlockSpec((1, d), lambda i: (0, 0))],
        out_specs=pl.BlockSpec((tm, d), lambda i: (i, 0)),
        out_shape=jax.ShapeDtypeStruct((t, d), F32),
        compiler_params=pltpu.CompilerParams(
            dimension_semantics=("parallel",),
            vmem_limit_bytes=_vmem_limit(2 * tm * k * 2, k * d * 2, 4 * tm * d * 4, 2 * tm * d * 4)),
        name="sb_outproj_ln",
    )(o, w, x, g, b)


def _ffn_up_kernel(x_ref, wg_ref, wu_ref, h_ref, xb_ref):
    @pl.when(pl.program_id(1) == 0)
    def _():
        xb_ref[...] = x_ref[...].astype(BF16)

    xb = xb_ref[...]
    gate = jnp.dot(xb, wg_ref[...], preferred_element_type=F32)
    up = jnp.dot(xb, wu_ref[...], preferred_element_type=F32)
    h_ref[...] = (jax.nn.silu(gate) * up).astype(h_ref.dtype)


def _ffn_up(x, w_gate_up, layer):
    t, d = x.shape
    f = w_gate_up.shape[2] // 2
    tm, tn = min(PROJ_ROWS, t), min(FFN_COLS, f)
    nj = f // tn
    return pl.pallas_call(
        _ffn_up_kernel,
        grid=(t // tm, nj),
        in_specs=[pl.BlockSpec((tm, d), lambda i, j: (i, 0)),
                  pl.BlockSpec((None, d, tn), lambda i, j: (layer, 0, j)),
                  pl.BlockSpec((None, d, tn), lambda i, j: (layer, 0, nj + j))],
        out_specs=pl.BlockSpec((tm, tn), lambda i, j: (i, j)),
        out_shape=jax.ShapeDtypeStruct((t, f), BF16),
        scratch_shapes=[pltpu.VMEM((tm, d), BF16)],
        compiler_params=pltpu.CompilerParams(
            dimension_semantics=("parallel", "arbitrary"),
            vmem_limit_bytes=_vmem_limit(2 * tm * d * 4, tm * d * 2, 4 * d * tn * 2, 2 * tm * tn * 2, 3 * tm * tn * 4)),
        name="ffn_up",
    )(x, w_gate_up, w_gate_up)


def _ffn_down_ln_kernel(h_ref, w_ref, x_ref, g_ref, b_ref, y_ref, *, alpha):
    _project_residual_ln(h_ref[...], w_ref, x_ref, g_ref, b_ref, y_ref, alpha)


def _ffn_down_ln(h, w, layer, x, g, b, alpha):
    t, f = h.shape
    d = w.shape[2]
    tm = min(FFN_DOWN_ROWS, t)
    return pl.pallas_call(
        functools.partial(_ffn_down_ln_kernel, alpha=alpha),
        grid=(t // tm,),
        in_specs=[pl.BlockSpec((tm, f), lambda i: (i, 0)),
                  pl.BlockSpec((None, f, d), lambda i: (layer, 0, 0), pipeline_mode=pl.Buffered(1)),
                  pl.BlockSpec((tm, d), lambda i: (i, 0)),
                  pl.BlockSpec((1, d), lambda i: (0, 0)),
                  pl.BlockSpec((1, d), lambda i: (0, 0))],
        out_specs=pl.BlockSpec((tm, d), lambda i: (i, 0)),
        out_shape=jax.ShapeDtypeStruct((t, d), F32),
        compiler_params=pltpu.CompilerParams(
            dimension_semantics=("parallel",),
            vmem_limit_bytes=_vmem_limit(f * d * 2, 2 * tm * f * 2, 4 * tm * d * 4, 2 * tm * d * 4)),
        name="ffn_down_ln",
    )(h, w, x, g, b)


def _rope_tables(seq):
    pos = jnp.arange(seq, dtype=F32)
    inv_freq = ROPE_THETA ** (-jnp.arange(0, HEAD_DIM, 2, dtype=F32) / HEAD_DIM)
    ang = pos[:, None] * inv_freq[None, :]
    cos, sin = jnp.cos(ang), jnp.sin(ang)
    return jnp.concatenate([cos, cos], axis=1), jnp.concatenate([-sin, sin], axis=1)


def _dil_proj_kernel(x_ref, wq_ref, wk_ref, wv_ref, cosq_ref, sinq_ref, cos_ref, sin_ref, q_ref, k_ref, v_ref,
                     xp_ref, *planes):
    _, r, tu, _ = q_ref.shape
    if r == 1:
        xp_ref[...] = x_ref[...].astype(BF16)
    else:
        src, dst = planes
        n_planes = src.shape[0]
        for j in range(n_planes):
            src[j] = x_ref[:, j * LANES:(j + 1) * LANES]
        blocks, mult, ways = [(0, src.shape[1], 0)], 1, r
        while ways > DEINTERLEAVE_STRIDE:
            split = []
            for off, n, base in blocks:
                part = n // DEINTERLEAVE_STRIDE
                for c in range(DEINTERLEAVE_STRIDE):
                    for j in range(n_planes):
                        dst[j, off + c * part:off + (c + 1) * part, :] = (
                            src[j, pl.ds(off + c, part, stride=DEINTERLEAVE_STRIDE), :])
                    split.append((off + c * part, part, base + c * mult))
            blocks, mult, ways = split, mult * DEINTERLEAVE_STRIDE, ways // DEINTERLEAVE_STRIDE
            src, dst = dst, src
        for off, n, base in blocks:
            for m in range(ways):
                c = base + m * mult
                for j in range(n_planes):
                    rows = src[j, pl.ds(off + m, tu, stride=ways), :]
                    xp_ref[c * tu:(c + 1) * tu, j * LANES:(j + 1) * LANES] = rows.astype(BF16)
    xb = xp_ref[...]
    for w_ref, out_ref, rope in ((wq_ref, q_ref, (cosq_ref, sinq_ref)), (wk_ref, k_ref, (cos_ref, sin_ref)),
                                 (wv_ref, v_ref, None)):
        acc = jnp.dot(xb, w_ref[...], preferred_element_type=F32)
        for h in range(out_ref.shape[0]):
            a = acc[:, h * HEAD_DIM:(h + 1) * HEAD_DIM]
            if rope:
                a = a * rope[0][...] + pltpu.roll(a, HEAD_DIM // 2, 1) * rope[1][...]
            a = a.astype(out_ref.dtype)
            for c in range(r):
                out_ref[h, c] = a[c * tu:(c + 1) * tu, :]


def _dil_proj(x, w, layer, group, dilation, cos2, sin2, batch, q_scale):
    t, d = x.shape
    s = t // batch
    r = dilation
    length = s // r
    tm = min(DIL_PROJ_ROWS, s)
    tu = tm // r
    nt = s // tm
    planes = [] if r == 1 else [pltpu.VMEM((d // LANES, tm, LANES), F32)] * 2
    n_groups = len(DIL_CONFIGS)
    gw = DIL_HEADS_PER_GROUP * HEAD_DIM
    cos_p = cos2.reshape(nt, tu, r, HEAD_DIM).transpose(0, 2, 1, 3).reshape(s, HEAD_DIM)
    sin_p = sin2.reshape(nt, tu, r, HEAD_DIM).transpose(0, 2, 1, 3).reshape(s, HEAD_DIM)
    w_spec = lambda kind: pl.BlockSpec((None, d, gw), lambda i: (layer, 0, kind * n_groups + group),
                                       pipeline_mode=pl.Buffered(1))
    tab_spec = pl.BlockSpec((tm, HEAD_DIM), lambda i: (i % nt, 0))
    out_spec = pl.BlockSpec((DIL_HEADS_PER_GROUP, None, r, tu, HEAD_DIM), lambda i: (0, i // nt, 0, i % nt, 0))
    out_sds = jax.ShapeDtypeStruct((DIL_HEADS_PER_GROUP, batch, r, length, HEAD_DIM), BF16)
    return pl.pallas_call(
        _dil_proj_kernel,
        grid=(t // tm,),
        in_specs=[pl.BlockSpec((tm, d), lambda i: (i, 0)),
                  w_spec(0), w_spec(1), w_spec(2), tab_spec, tab_spec, tab_spec, tab_spec],
        out_specs=[out_spec, out_spec, out_spec],
        out_shape=[out_sds, out_sds, out_sds],
        scratch_shapes=[pltpu.VMEM((tm, d), BF16)] + planes,
        compiler_params=pltpu.CompilerParams(
            dimension_semantics=("parallel",),
            vmem_limit_bytes=_vmem_limit(4 * tm * d * 4, tm * d * 2, 3 * d * gw * 2, 6 * tm * gw * 2, 2 * tm * gw * 4)),
        name=f"dil_qkv_proj_r{r}",
    )(x, w, w, w, cos_p * q_scale, sin_p * q_scale, cos_p, sin_p)


def _dil_attn_kernel(q_ref, k_ref, v_ref, kp_ref, vp_ref, o_ref, lse_ref):
    n_heads, n_res, tu, _ = q_ref.shape
    sub = DIL_SUB
    iu = pl.program_id(2)
    qi = lax.broadcasted_iota(jnp.int32, (sub, 2 * sub), 0)
    kj = lax.broadcasted_iota(jnp.int32, (sub, 2 * sub), 1)
    in_window = (kj >= qi) & (kj <= qi + sub)
    first_valid = in_window & (kj >= jnp.where(iu > 0, 0, sub))
    lane = lax.broadcasted_iota(jnp.int32, (sub, LANES), 1)
    for c in range(n_res):
        for s in range(tu // sub):
            rows = slice(s * sub, (s + 1) * sub)
            valid = first_valid if s == 0 else in_window
            lse_tile = jnp.zeros((sub, LANES), F32)
            for h in range(n_heads):
                q = q_ref[h, c, rows, :]
                if s == 0:
                    k_prev, v_prev = kp_ref[h, c], vp_ref[h, c]
                else:
                    k_prev, v_prev = k_ref[h, c, (s - 1) * sub:s * sub, :], v_ref[h, c, (s - 1) * sub:s * sub, :]
                keys = jnp.concatenate([k_prev, k_ref[h, c, rows, :]], axis=0)
                vals = jnp.concatenate([v_prev, v_ref[h, c, rows, :]], axis=0)
                sc = lax.dot_general(q, keys, (((1,), (1,)), ((), ())), preferred_element_type=F32)
                sc = jnp.where(valid, sc, -jnp.inf)
                m = jnp.max(sc, axis=1, keepdims=True)
                p = jnp.exp2(sc - m)
                den = jnp.sum(p, axis=1, keepdims=True)
                o = jnp.dot(p.astype(BF16), vals, preferred_element_type=F32) / den
                o_ref[c, rows, h * HEAD_DIM:(h + 1) * HEAD_DIM] = o.astype(o_ref.dtype)
                lse_tile = jnp.where(lane == h, m + jnp.log(den) * LOG2_E, lse_tile)
            lse_ref[c, rows, :] = lse_tile


def _dil_attention(q, k, v):
    n_heads, b, r, length, _ = q.shape
    tu = min(DIL_Q_ROWS, length)
    rc = min(r, DIL_Q_ROWS // tu)
    sub = DIL_SUB
    gw = n_heads * HEAD_DIM
    cur = pl.BlockSpec((n_heads, None, rc, tu, HEAD_DIM), lambda bi, c, iu: (0, bi, c, iu, 0))
    prev = pl.BlockSpec((n_heads, None, rc, sub, HEAD_DIM),
                        lambda bi, c, iu: (0, bi, c, jnp.maximum(iu * (tu // sub) - 1, 0), 0))
    return pl.pallas_call(
        _dil_attn_kernel,
        grid=(b, r // rc, length // tu),
        in_specs=[cur, cur, cur, prev, prev],
        out_specs=[pl.BlockSpec((None, rc, tu, gw), lambda bi, c, iu: (bi, c, iu, 0)),
                   pl.BlockSpec((None, rc, tu, LANES), lambda bi, c, iu: (bi, c, iu, 0))],
        out_shape=[jax.ShapeDtypeStruct((b, r, length, gw), BF16),
                   jax.ShapeDtypeStruct((b, r, length, LANES), F32)],
        compiler_params=pltpu.CompilerParams(
            dimension_semantics=("parallel", "parallel", "parallel"),
            vmem_limit_bytes=_vmem_limit(6 * rc * tu * gw * 2, 4 * rc * sub * gw * 2, 2 * rc * tu * gw * 2,
                                         2 * rc * tu * LANES * 4)),
        name=f"dil_attention_r{r}",
    )(q, k, v, k, v)


def _dil_outproj_ln_kernel(o0_ref, o1_ref, o2_ref, l0_ref, l1_ref, l2_ref, w_ref, x_ref, g_ref, b_ref, y_ref,
                           planes_ref, *, alpha):
    o_refs, l_refs = (o0_ref, o1_ref, o2_ref), (l0_ref, l1_ref, l2_ref)
    n_o = len(o_refs) * DIL_HEADS_PER_GROUP

    def natural(ref, cols, plane):
        r, tu, _ = ref.shape
        if r == 1:
            return ref[0, :, cols].astype(F32)
        if r <= DEINTERLEAVE_STRIDE:
            for c in range(r):
                planes_ref[plane, pl.ds(c, tu, stride=r), :] = ref[c, :, cols].astype(F32)
            return planes_ref[plane]
        narrow, wide = DEINTERLEAVE_STRIDE, r // DEINTERLEAVE_STRIDE
        stage, part = planes_ref.shape[0] - 1, tu * wide
        for c in range(r):
            lo, hi = c % narrow, c // narrow
            planes_ref[stage, pl.ds(lo * part + hi, tu, stride=wide), :] = ref[c, :, cols].astype(F32)
        for lo in range(narrow):
            planes_ref[plane, pl.ds(lo, part, stride=narrow), :] = planes_ref[stage, lo * part:(lo + 1) * part, :]
        return planes_ref[plane]

    lses = [natural(l_ref, slice(0, LANES), n_o + g) for g, l_ref in enumerate(l_refs)]
    mx = jnp.maximum(jnp.maximum(lses[0], lses[1]), lses[2])
    es = [jnp.exp2(l - mx) for l in lses]
    den = es[0] + es[1] + es[2]
    pieces = []
    for g, (o_ref, e) in enumerate(zip(o_refs, es)):
        share = e / den
        for h in range(DIL_HEADS_PER_GROUP):
            o = natural(o_ref, slice(h * HEAD_DIM, (h + 1) * HEAD_DIM), g * DIL_HEADS_PER_GROUP + h)
            pieces.append((o * share[:, h:h + 1]).astype(BF16))
    _project_residual_ln(jnp.concatenate(pieces, axis=1), w_ref, x_ref, g_ref, b_ref, y_ref, alpha)


def _dil_outproj_ln(os, lses, w, layer, x, g, b, alpha):
    t, d = x.shape
    _, k, _ = w.shape
    batch = os[0].shape[0]
    tm = min(LN_ROWS, t // batch)
    nt = t // batch // tm

    def res_spec(a):
        _, r, _, width = a.shape
        return pl.BlockSpec((None, r, tm // r, width), lambda i: (i // nt, 0, i % nt, 0))

    n_planes = len(os) * DIL_HEADS_PER_GROUP + len(lses) + 1
    return pl.pallas_call(
        functools.partial(_dil_outproj_ln_kernel, alpha=alpha),
        grid=(t // tm,),
        in_specs=[res_spec(a) for a in (*os, *lses)] + [
                  pl.BlockSpec((None, k, d), lambda i: (layer, 0, 0), pipeline_mode=pl.Buffered(1)),
                  pl.BlockSpec((tm, d), lambda i: (i, 0)),
                  pl.BlockSpec((1, d), lambda i: (0, 0)),
                  pl.BlockSpec((1, d), lambda i: (0, 0))],
        out_specs=pl.BlockSpec((tm, d), lambda i: (i, 0)),
        out_shape=jax.ShapeDtypeStruct((t, d), F32),
        scratch_shapes=[pltpu.VMEM((n_planes, tm, LANES), F32)],
        compiler_params=pltpu.CompilerParams(
            dimension_semantics=("parallel",),
            vmem_limit_bytes=_vmem_limit(3 * tm * k * 2, k * d * 2, 4 * tm * d * 4, 2 * tm * d * 4,
                                         n_planes * tm * LANES * 4)),
        name="dil_outproj_ln",
    )(*os, *lses, w, x, g, b)


def kernel(x, w_qkv_sb, w_o_sb, w_qkv_dil, w_o_dil, ln_mix_g, ln_mix_b, w_gate_up, w_down, ln_ffn_g, ln_ffn_b):
    batch, seq, d_model = x.shape
    depth = ln_mix_g.shape[0]
    alpha = (2.0 * depth) ** 0.25
    t = batch * seq
    sb_heads = w_o_sb.shape[1] // HEAD_DIM
    assert all(window // dilation == DIL_SUB for window, dilation in DIL_CONFIGS)

    w_qkv_first = w_qkv_sb[:1].astype(BF16)
    later_weights = ((w_qkv_sb, 1), (w_o_sb, 0), (w_qkv_dil, 0), (w_o_dil, 0), (w_gate_up, 0), (w_down, 0))
    cos2, sin2 = _rope_tables(seq)
    q_scale = LOG2_E / math.sqrt(HEAD_DIM)

    xf = x.reshape(t, d_model)
    for layer in range(depth):
        j = layer // 2
        g_mix, b_mix = ln_mix_g[layer][None, :], ln_mix_b[layer][None, :]
        if layer % 2 == 0:
            qkv = _proj_heads(xf, w_qkv_first, 0, q_scale) if layer == 0 else _proj_heads(xf, w_qkv_rest, j - 1, q_scale)
            o, rounded = _sb_attention(qkv, batch, seq, sb_heads, later_weights if layer == 0 else ())
            if layer == 0:
                w_qkv_rest, w_o_sb, w_qkv_dil, w_o_dil, w_gate_up, w_down = rounded
            xf = _outproj_ln(o, w_o_sb, j, xf, g_mix, b_mix, alpha)
        else:
            os, lses = [], []
            for group, (_, dilation) in enumerate(DIL_CONFIGS):
                q, k, v = _dil_proj(xf, w_qkv_dil, j, group, dilation, cos2, sin2, batch, q_scale)
                o, lse = _dil_attention(q, k, v)
                os.append(o)
                lses.append(lse)
            xf = _dil_outproj_ln(os, lses, w_o_dil, j, xf, g_mix, b_mix, alpha)
        h = _ffn_up(xf, w_gate_up, layer)
        xf = _ffn_down_ln(h, w_down, layer, xf, ln_ffn_g[layer][None, :], ln_ffn_b[layer][None, :], alpha)
    return xf.reshape(batch, seq, d_model)
```

```python
import functools
import math

import jax
import jax.numpy as jnp
from jax import lax
from jax.experimental import pallas as pl
from jax.experimental.pallas import tpu as pltpu

HEAD_DIM = 128
DIL_CONFIGS = ((128, 1), (512, 4), (2048, 16))
DIL_HEADS_PER_GROUP = 6
ROPE_THETA = 10000.0
LN_EPS = 1e-5
LOG2_E = math.log2(math.e)

V7X_VMEM_BYTES = 64 * 1024 * 1024
V7X_MXU_DIM = 256
LANES = 128
BF16_SUBLANES = 16
SB_MAX_LOGIT2 = 126.0
PROJ_ROWS = 1024
PROJ_COLS = 1024
LN_ROWS = 512
LN_CHUNK_ROWS = 128
FFN_COLS = 512
FFN_DOWN_ROWS = 512
SB_Q_ROWS = 512
SB_KEY_BLOCK = V7X_MXU_DIM
SB_HEADS_PER_STEP = 4
DIL_PROJ_ROWS = 512
DEINTERLEAVE_STRIDE = 4
DIL_Q_ROWS = 1024
DIL_SUB = 128

BF16 = jnp.bfloat16
F32 = jnp.float32


def _vmem_limit(*nbytes):
    return int(min(V7X_VMEM_BYTES * 7 // 8, max(sum(nbytes) * 3 // 2, V7X_VMEM_BYTES // 4)))


def _layer_norm_rows(y, g, b):
    mu = jnp.mean(y, axis=-1, keepdims=True)
    d = y - mu
    var = jnp.mean(d * d, axis=-1, keepdims=True)
    return d * lax.rsqrt(var + LN_EPS) * g + b


def _proj_heads_kernel(x_ref, w_ref, o_ref, xb_ref, *, q_tiles, q_scale):
    j = pl.program_id(1)

    @pl.when(j == 0)
    def _():
        xb_ref[...] = x_ref[...].astype(BF16)

    acc = jnp.dot(xb_ref[...], w_ref[...], preferred_element_type=F32)
    acc = acc * jnp.where(j < q_tiles, q_scale, 1.0)
    for h in range(o_ref.shape[0]):
        o_ref[h] = acc[:, h * HEAD_DIM:(h + 1) * HEAD_DIM].astype(o_ref.dtype)


def _proj_heads(x, w, layer, q_scale):
    t, k = x.shape
    n = w.shape[2]
    tm, tn = min(PROJ_ROWS, t), min(PROJ_COLS, n // 3)
    hpt = tn // HEAD_DIM
    return pl.pallas_call(
        functools.partial(_proj_heads_kernel, q_tiles=n // 3 // tn, q_scale=q_scale),
        grid=(t // tm, n // tn),
        in_specs=[pl.BlockSpec((tm, k), lambda i, j: (i, 0)),
                  pl.BlockSpec((None, k, tn), lambda i, j: (layer, 0, j))],
        out_specs=pl.BlockSpec((hpt, tm, HEAD_DIM), lambda i, j: (j, i, 0)),
        out_shape=jax.ShapeDtypeStruct((n // HEAD_DIM, t, HEAD_DIM), BF16),
        scratch_shapes=[pltpu.VMEM((tm, k), BF16)],
        compiler_params=pltpu.CompilerParams(
            dimension_semantics=("parallel", "arbitrary"),
            vmem_limit_bytes=_vmem_limit(2 * tm * k * 4, tm * k * 2, 2 * k * tn * 2, 2 * tm * tn * 2, tm * tn * 4)),
        name="sb_qkv_proj",
    )(x, w)


def _sb_attn_kernel(q_ref, k_ref, v_ref, tri_ref, *rest):
    n_cast = (len(rest) - 3) // 2
    cast_in, o_ref, cast_out = rest[:n_cast], rest[n_cast], rest[n_cast + 1:2 * n_cast + 1]
    acc_ref, run_ref = rest[2 * n_cast + 1:]
    for src, dst in zip(cast_in, cast_out):
        dst[...] = src[...].astype(dst.dtype)
    n_heads, tq, _ = q_ref.shape
    tk = SB_KEY_BLOCK
    diag_blocks = tq // tk
    i = pl.program_id(2)
    acc_ref[...] = jnp.zeros_like(acc_ref)
    run_ref[...] = jnp.zeros_like(run_ref)

    def visit(k0, row0, n_blocks, masked):
        heads = range(n_heads)
        zs, costs = [], []
        for h in heads:
            q = q_ref[h, row0:, :]
            kblk = k_ref[h, pl.ds(k0, n_blocks * tk), :]
            zs.append(lax.dot_general(q, kblk, (((1,), (1,)), ((), ())), preferred_element_type=F32))
        if masked:
            qpos = i * tq + row0 + lax.broadcasted_iota(jnp.int32, zs[0].shape, 0)
            kpos = k0 + lax.broadcasted_iota(jnp.int32, zs[0].shape, 1)
            causal = kpos < qpos
        for z in zs:
            cost = jnp.maximum(z, jnp.log(1.0 + jnp.exp2(jnp.minimum(z, SB_MAX_LOGIT2))) * LOG2_E)
            if masked:
                cost = jnp.where(causal, cost, 0.0)
            costs.append(cost.astype(BF16))
        runs = [run_ref[h, row0:, :] for h in heads]
        pieces = [[None] * n_blocks for _ in heads]
        for s in reversed(range(n_blocks)):
            cols = slice(s * tk, (s + 1) * tk)
            for h in heads:
                incl = jnp.dot(costs[h][:, cols], tri_ref[...], preferred_element_type=F32)
                pieces[h][s] = jnp.exp2(zs[h][:, cols] - incl - runs[h])
                runs[h] = runs[h] + incl[:, 0:1]
        for h in heads:
            w = pieces[h][0] if n_blocks == 1 else jnp.concatenate(pieces[h], axis=1)
            if masked:
                w = jnp.where(causal, w, 0.0)
            vblk = v_ref[h, pl.ds(k0, n_blocks * tk), :]
            acc_ref[h, row0:, :] += jnp.dot(w.astype(BF16), vblk, preferred_element_type=F32)
            run_ref[h, row0:, :] = runs[h]

    diag0 = pl.multiple_of(i * tq, tq)
    for d in reversed(range(diag_blocks)):
        visit(diag0 + d * tk, d * tk, 1, True)

    def body(n, carry):
        visit(pl.multiple_of((i - 1 - n) * tq, tq), 0, diag_blocks, False)
        return carry

    lax.fori_loop(0, i, body, 0)
    o_ref[...] = acc_ref[...].astype(o_ref.dtype)


def _cast_slab_rows(rows, n_steps):
    slab = -(-rows // n_steps)
    slab += -slab % BF16_SUBLANES
    while rows % slab:
        slab += BF16_SUBLANES
    return slab


def _sb_attention(qkv, batch, seq, n_heads, weights=()):
    tq, tk = min(SB_Q_ROWS, seq), SB_KEY_BLOCK
    nq = seq // tq
    hp = SB_HEADS_PER_STEP
    ng = n_heads // hp
    n_steps = ng * batch * nq
    tri = (lax.broadcasted_iota(jnp.int32, (tk, tk), 0) >= lax.broadcasted_iota(jnp.int32, (tk, tk), 1)).astype(BF16)
    flat = [w.reshape(-1, w.shape[-1]) for w, _ in weights]
    out_shapes = [(w.shape[0] - skip,) + w.shape[1:] for w, skip in weights]
    in_slabs, out_slabs, slab_bytes = [], [], 0
    for (w, skip), out_shape in zip(weights, out_shapes):
        rows, cols = out_shape[0] * out_shape[1], out_shape[2]
        slab = _cast_slab_rows(rows, n_steps)
        assert w.shape[1] % slab == 0
        last, first = rows // slab - 1, skip * w.shape[1] // slab
        step = lambda g, b, i, last=last: jnp.minimum((g * batch + b) * nq + i, last)
        in_slabs.append(pl.BlockSpec((slab, cols), lambda g, b, i, step=step, first=first: (first + step(g, b, i), 0)))
        out_slabs.append(pl.BlockSpec((slab, cols), lambda g, b, i, step=step: (step(g, b, i), 0)))
        slab_bytes += 2 * slab * cols * (4 + 2)
    outs = pl.pallas_call(
        _sb_attn_kernel,
        grid=(ng, batch, nq),
        in_specs=[pl.BlockSpec((hp, tq, HEAD_DIM), lambda g, b, i: (g, b * nq + i, 0)),
                  pl.BlockSpec((hp, seq, HEAD_DIM), lambda g, b, i: (ng + g, b, 0)),
                  pl.BlockSpec((hp, seq, HEAD_DIM), lambda g, b, i: (2 * ng + g, b, 0)),
                  pl.BlockSpec((tk, tk), lambda g, b, i: (0, 0))] + in_slabs,
        out_specs=[pl.BlockSpec((hp, tq, HEAD_DIM), lambda g, b, i: (g, b * nq + i, 0))] + out_slabs,
        out_shape=[jax.ShapeDtypeStruct((n_heads, batch * seq, HEAD_DIM), BF16)] +
                  [jax.ShapeDtypeStruct((s[0] * s[1], s[2]), BF16) for s in out_shapes],
        scratch_shapes=[pltpu.VMEM((hp, tq, HEAD_DIM), F32), pltpu.VMEM((hp, tq, 1), F32)],
        compiler_params=pltpu.CompilerParams(
            dimension_semantics=("arbitrary", "arbitrary", "arbitrary"),
            vmem_limit_bytes=_vmem_limit(4 * hp * seq * HEAD_DIM * 2, 8 * hp * tq * tq * 4, slab_bytes)),
        name="sb_attention",
    )(qkv, qkv, qkv, tri, *flat)
    return outs[0], [o.reshape(s) for o, s in zip(outs[1:], out_shapes)]


def _project_residual_ln(lhs, w_ref, x_ref, g_ref, b_ref, y_ref, alpha):
    rows = min(LN_CHUNK_ROWS, lhs.shape[0])
    for c in range(lhs.shape[0] // rows):
        sl = slice(c * rows, (c + 1) * rows)
        mixed = jnp.dot(lhs[sl], w_ref[...], preferred_element_type=F32)
        y_ref[sl, :] = _layer_norm_rows(alpha * x_ref[sl, :] + mixed, g_ref[...], b_ref[...])


def _outproj_ln_kernel(o_ref, w_ref, x_ref, g_ref, b_ref, y_ref, *, alpha):
    lhs = jnp.concatenate([o_ref[h] for h in range(o_ref.shape[0])], axis=1)
    _project_residual_ln(lhs, w_ref, x_ref, g_ref, b_ref, y_ref, alpha)


def _outproj_ln(o, w, layer, x, g, b, alpha):
    n_heads, t, _ = o.shape
    _, k, d = w.shape
    tm = min(LN_ROWS, t)
    return pl.pallas_call(
        functools.partial(_outproj_ln_kernel, alpha=alpha),
        grid=(t // tm,),
        in_specs=[pl.BlockSpec((n_heads, tm, HEAD_DIM), lambda i: (0, i, 0)),
                  pl.BlockSpec((None, k, d), lambda i: (layer, 0, 0), pipeline_mode=pl.Buffered(1)),
                  pl.BlockSpec((tm, d), lambda i: (i, 0)),
                  pl.BlockSpec((1, d), lambda i: (0, 0)),
                  pl.BlockSpec((1, d), lambda i: (0, 0))],
        out_specs=pl.BlockSpec((tm, d), lambda i: (i, 0)),
        out_shape=jax.ShapeDtypeStruct((t, d), F32),
        compiler_params=pltpu.CompilerParams(
            dimension_semantics=("parallel",),
            vmem_limit_bytes=_vmem_limit(2 * tm * k * 2, k * d * 2, 4 * tm * d * 4, 2 * tm * d * 4)),
        name="sb_outproj_ln",
    )(o, w, x, g, b)


def _ffn_up_kernel(x_ref, wg_ref, wu_ref, h_ref, xb_ref):
    @pl.when(pl.program_id(1) == 0)
    def _():
        xb_ref[...] = x_ref[...].astype(BF16)

    xb = xb_ref[...]
    gate = jnp.dot(xb, wg_ref[...], preferred_element_type=F32)
    up = jnp.dot(xb, wu_ref[...], preferred_element_type=F32)
    h_ref[...] = (jax.nn.silu(gate) * up).astype(h_ref.dtype)


def _ffn_up(x, w_gate_up, layer):
    t, d = x.shape
    f = w_gate_up.shape[2] // 2
    tm, tn = min(PROJ_ROWS, t), min(FFN_COLS, f)
    nj = f // tn
    return pl.pallas_call(
        _ffn_up_kernel,
        grid=(t // tm, nj),
        in_specs=[pl.BlockSpec((tm, d), lambda i, j: (i, 0)),
                  pl.BlockSpec((None, d, tn), lambda i, j: (layer, 0, j)),
                  pl.BlockSpec((None, d, tn), lambda i, j: (layer, 0, nj + j))],
        out_specs=pl.BlockSpec((tm, tn), lambda i, j: (i, j)),
        out_shape=jax.ShapeDtypeStruct((t, f), BF16),
        scratch_shapes=[pltpu.VMEM((tm, d), BF16)],
        compiler_params=pltpu.CompilerParams(
            dimension_semantics=("parallel", "arbitrary"),
            vmem_limit_bytes=_vmem_limit(2 * tm * d * 4, tm * d * 2, 4 * d * tn * 2, 2 * tm * tn * 2, 3 * tm * tn * 4)),
        name="ffn_up",
    )(x, w_gate_up, w_gate_up)


def _ffn_down_ln_kernel(h_ref, w_ref, x_ref, g_ref, b_ref, y_ref, *, alpha):
    _project_residual_ln(h_ref[...], w_ref, x_ref, g_ref, b_ref, y_ref, alpha)


def _ffn_down_ln(h, w, layer, x, g, b, alpha):
    t, f = h.shape
    d = w.shape[2]
    tm = min(FFN_DOWN_ROWS, t)
    return pl.pallas_call(
        functools.partial(_ffn_down_ln_kernel, alpha=alpha),
        grid=(t // tm,),
        in_specs=[pl.BlockSpec((tm, f), lambda i: (i, 0)),
                  pl.BlockSpec((None, f, d), lambda i: (layer, 0, 0), pipeline_mode=pl.Buffered(1)),
                  pl.BlockSpec((tm, d), lambda i: (i, 0)),
                  pl.BlockSpec((1, d), lambda i: (0, 0)),
                  pl.BlockSpec((1, d), lambda i: (0, 0))],
        out_specs=pl.BlockSpec((tm, d), lambda i: (i, 0)),
        out_shape=jax.ShapeDtypeStruct((t, d), F32),
        compiler_params=pltpu.CompilerParams(
            dimension_semantics=("parallel",),
            vmem_limit_bytes=_vmem_limit(f * d * 2, 2 * tm * f * 2, 4 * tm * d * 4, 2 * tm * d * 4)),
        name="ffn_down_ln",
    )(h, w, x, g, b)


def _rope_tables(seq):
    pos = jnp.arange(seq, dtype=F32)
    inv_freq = ROPE_THETA ** (-jnp.arange(0, HEAD_DIM, 2, dtype=F32) / HEAD_DIM)
    ang = pos[:, None] * inv_freq[None, :]
    cos, sin = jnp.cos(ang), jnp.sin(ang)
    return jnp.concatenate([cos, cos], axis=1), jnp.concatenate([-sin, sin], axis=1)


def _dil_proj_kernel(x_ref, wq_ref, wk_ref, wv_ref, cosq_ref, sinq_ref, cos_ref, sin_ref, q_ref, k_ref, v_ref,
                     xp_ref, *planes):
    _, r, tu, _ = q_ref.shape
    if r == 1:
        xp_ref[...] = x_ref[...].astype(BF16)
    else:
        src, dst = planes
        n_planes = src.shape[0]
        for j in range(n_planes):
            src[j] = x_ref[:, j * LANES:(j + 1) * LANES]
        blocks, mult, ways = [(0, src.shape[1], 0)], 1, r
        while ways > DEINTERLEAVE_STRIDE:
            split = []
            for off, n, base in blocks:
                part = n // DEINTERLEAVE_STRIDE
                for c in range(DEINTERLEAVE_STRIDE):
                    for j in range(n_planes):
                        dst[j, off + c * part:off + (c + 1) * part, :] = (
                            src[j, pl.ds(off + c, part, stride=DEINTERLEAVE_STRIDE), :])
                    split.append((off + c * part, part, base + c * mult))
            blocks, mult, ways = split, mult * DEINTERLEAVE_STRIDE, ways // DEINTERLEAVE_STRIDE
            src, dst = dst, src
        for off, n, base in blocks:
            for m in range(ways):
                c = base + m * mult
                for j in range(n_planes):
                    rows = src[j, pl.ds(off + m, tu, stride=ways), :]
                    xp_ref[c * tu:(c + 1) * tu, j * LANES:(j + 1) * LANES] = rows.astype(BF16)
    xb = xp_ref[...]
    for w_ref, out_ref, rope in ((wq_ref, q_ref, (cosq_ref, sinq_ref)), (wk_ref, k_ref, (cos_ref, sin_ref)),
                                 (wv_ref, v_ref, None)):
        acc = jnp.dot(xb, w_ref[...], preferred_element_type=F32)
        for h in range(out_ref.shape[0]):
            a = acc[:, h * HEAD_DIM:(h + 1) * HEAD_DIM]
            if rope:
                a = a * rope[0][...] + pltpu.roll(a, HEAD_DIM // 2, 1) * rope[1][...]
            a = a.astype(out_ref.dtype)
            for c in range(r):
                out_ref[h, c] = a[c * tu:(c + 1) * tu, :]


def _dil_proj(x, w, layer, group, dilation, cos2, sin2, batch, q_scale):
    t, d = x.shape
    s = t // batch
    r = dilation
    length = s // r
    tm = min(DIL_PROJ_ROWS, s)
    tu = tm // r
    nt = s // tm
    planes = [] if r == 1 else [pltpu.VMEM((d // LANES, tm, LANES), F32)] * 2
    n_groups = len(DIL_CONFIGS)
    gw = DIL_HEADS_PER_GROUP * HEAD_DIM
    cos_p = cos2.reshape(nt, tu, r, HEAD_DIM).transpose(0, 2, 1, 3).reshape(s, HEAD_DIM)
    sin_p = sin2.reshape(nt, tu, r, HEAD_DIM).transpose(0, 2, 1, 3).reshape(s, HEAD_DIM)
    w_spec = lambda kind: pl.BlockSpec((None, d, gw), lambda i: (layer, 0, kind * n_groups + group),
                                       pipeline_mode=pl.Buffered(1))
    tab_spec = pl.BlockSpec((tm, HEAD_DIM), lambda i: (i % nt, 0))
    out_spec = pl.BlockSpec((DIL_HEADS_PER_GROUP, None, r, tu, HEAD_DIM), lambda i: (0, i // nt, 0, i % nt, 0))
    out_sds = jax.ShapeDtypeStruct((DIL_HEADS_PER_GROUP, batch, r, length, HEAD_DIM), BF16)
    return pl.pallas_call(
        _dil_proj_kernel,
        grid=(t // tm,),
        in_specs=[pl.BlockSpec((tm, d), lambda i: (i, 0)),
                  w_spec(0), w_spec(1), w_spec(2), tab_spec, tab_spec, tab_spec, tab_spec],
        out_specs=[out_spec, out_spec, out_spec],
        out_shape=[out_sds, out_sds, out_sds],
        scratch_shapes=[pltpu.VMEM((tm, d), BF16)] + planes,
        compiler_params=pltpu.CompilerParams(
            dimension_semantics=("parallel",),
            vmem_limit_bytes=_vmem_limit(4 * tm * d * 4, tm * d * 2, 3 * d * gw * 2, 6 * tm * gw * 2, 2 * tm * gw * 4)),
        name=f"dil_qkv_proj_r{r}",
    )(x, w, w, w, cos_p * q_scale, sin_p * q_scale, cos_p, sin_p)


def _dil_attn_kernel(q_ref, k_ref, v_ref, kp_ref, vp_ref, o_ref, lse_ref):
    n_heads, n_res, tu, _ = q_ref.shape
    sub = DIL_SUB
    iu = pl.program_id(2)
    qi = lax.broadcasted_iota(jnp.int32, (sub, 2 * sub), 0)
    kj = lax.broadcasted_iota(jnp.int32, (sub, 2 * sub), 1)
    in_window = (kj >= qi) & (kj <= qi + sub)
    first_valid = in_window & (kj >= jnp.where(iu > 0, 0, sub))
    lane = lax.broadcasted_iota(jnp.int32, (sub, LANES), 1)
    for c in range(n_res):
        for s in range(tu // sub):
            rows = slice(s * sub, (s + 1) * sub)
            valid = first_valid if s == 0 else in_window
            lse_tile = jnp.zeros((sub, LANES), F32)
            for h in range(n_heads):
                q = q_ref[h, c, rows, :]
                if s == 0:
                    k_prev, v_prev = kp_ref[h, c], vp_ref[h, c]
                else:
                    k_prev, v_prev = k_ref[h, c, (s - 1) * sub:s * sub, :], v_ref[h, c, (s - 1) * sub:s * sub, :]
                keys = jnp.concatenate([k_prev, k_ref[h, c, rows, :]], axis=0)
                vals = jnp.concatenate([v_prev, v_ref[h, c, rows, :]], axis=0)
                sc = lax.dot_general(q, keys, (((1,), (1,)), ((), ())), preferred_element_type=F32)
                sc = jnp.where(valid, sc, -jnp.inf)
                m = jnp.max(sc, axis=1, keepdims=True)
                p = jnp.exp2(sc - m)
                den = jnp.sum(p, axis=1, keepdims=True)
                o = jnp.dot(p.astype(BF16), vals, preferred_element_type=F32) / den
                o_ref[c, rows, h * HEAD_DIM:(h + 1) * HEAD_DIM] = o.astype(o_ref.dtype)
                lse_tile = jnp.where(lane == h, m + jnp.log(den) * LOG2_E, lse_tile)
            lse_ref[c, rows, :] = lse_tile


def _dil_attention(q, k, v):
    n_heads, b, r, length, _ = q.shape
    tu = min(DIL_Q_ROWS, length)
    rc = min(r, DIL_Q_ROWS // tu)
    sub = DIL_SUB
    gw = n_heads * HEAD_DIM
    cur = pl.BlockSpec((n_heads, None, rc, tu, HEAD_DIM), lambda bi, c, iu: (0, bi, c, iu, 0))
    prev = pl.BlockSpec((n_heads, None, rc, sub, HEAD_DIM),
                        lambda bi, c, iu: (0, bi, c, jnp.maximum(iu * (tu // sub) - 1, 0), 0))
    return pl.pallas_call(
        _dil_attn_kernel,
        grid=(b, r // rc, length // tu),
        in_specs=[cur, cur, cur, prev, prev],
        out_specs=[pl.BlockSpec((None, rc, tu, gw), lambda bi, c, iu: (bi, c, iu, 0)),
                   pl.BlockSpec((None, rc, tu, LANES), lambda bi, c, iu: (bi, c, iu, 0))],
        out_shape=[jax.ShapeDtypeStruct((b, r, length, gw), BF16),
                   jax.ShapeDtypeStruct((b, r, length, LANES), F32)],
        compiler_params=pltpu.CompilerParams(
            dimension_semantics=("parallel", "parallel", "parallel"),
            vmem_limit_bytes=_vmem_limit(6 * rc * tu * gw * 2, 4 * rc * sub * gw * 2, 2 * rc * tu * gw * 2,
                                         2 * rc * tu * LANES * 4)),
        name=f"dil_attention_r{r}",
    )(q, k, v, k, v)


def _dil_outproj_ln_kernel(o0_ref, o1_ref, o2_ref, l0_ref, l1_ref, l2_ref, w_ref, x_ref, g_ref, b_ref, y_ref,
                           planes_ref, *, alpha):
    o_refs, l_refs = (o0_ref, o1_ref, o2_ref), (l0_ref, l1_ref, l2_ref)
    n_o = len(o_refs) * DIL_HEADS_PER_GROUP

    def natural(ref, cols, plane):
        r, tu, _ = ref.shape
        if r == 1:
            return ref[0, :, cols].astype(F32)
        if r <= DEINTERLEAVE_STRIDE:
            for c in range(r):
                planes_ref[plane, pl.ds(c, tu, stride=r), :] = ref[c, :, cols].astype(F32)
            return planes_ref[plane]
        narrow, wide = DEINTERLEAVE_STRIDE, r // DEINTERLEAVE_STRIDE
        stage, part = planes_ref.shape[0] - 1, tu * wide
        for c in range(r):
            lo, hi = c % narrow, c // narrow
            planes_ref[stage, pl.ds(lo * part + hi, tu, stride=wide), :] = ref[c, :, cols].astype(F32)
        for lo in range(narrow):
            planes_ref[plane, pl.ds(lo, part, stride=narrow), :] = planes_ref[stage, lo * part:(lo + 1) * part, :]
        return planes_ref[plane]

    lses = [natural(l_ref, slice(0, LANES), n_o + g) for g, l_ref in enumerate(l_refs)]
    mx = jnp.maximum(jnp.maximum(lses[0], lses[1]), lses[2])
    es = [jnp.exp2(l - mx) for l in lses]
    den = es[0] + es[1] + es[2]
    pieces = []
    for g, (o_ref, e) in enumerate(zip(o_refs, es)):
        share = e / den
        for h in range(DIL_HEADS_PER_GROUP):
            o = natural(o_ref, slice(h * HEAD_DIM, (h + 1) * HEAD_DIM), g * DIL_HEADS_PER_GROUP + h)
            pieces.append((o * share[:, h:h + 1]).astype(BF16))
    _project_residual_ln(jnp.concatenate(pieces, axis=1), w_ref, x_ref, g_ref, b_ref, y_ref, alpha)


def _dil_outproj_ln(os, lses, w, layer, x, g, b, alpha):
    t, d = x.shape
    _, k, _ = w.shape
    batch = os[0].shape[0]
    tm = min(LN_ROWS, t // batch)
    nt = t // batch // tm

    def res_spec(a):
        _, r, _, width = a.shape
        return pl.BlockSpec((None, r, tm // r, width), lambda i: (i // nt, 0, i % nt, 0))

    n_planes = len(os) * DIL_HEADS_PER_GROUP + len(lses) + 1
    return pl.pallas_call(
        functools.partial(_dil_outproj_ln_kernel, alpha=alpha),
        grid=(t // tm,),
        in_specs=[res_spec(a) for a in (*os, *lses)] + [
                  pl.BlockSpec((None, k, d), lambda i: (layer, 0, 0), pipeline_mode=pl.Buffered(1)),
                  pl.BlockSpec((tm, d), lambda i: (i, 0)),
                  pl.BlockSpec((1, d), lambda i: (0, 0)),
                  pl.BlockSpec((1, d), lambda i: (0, 0))],
        out_specs=pl.BlockSpec((tm, d), lambda i: (i, 0)),
        out_shape=jax.ShapeDtypeStruct((t, d), F32),
        scratch_shapes=[pltpu.VMEM((n_planes, tm, LANES), F32)],
        compiler_params=pltpu.CompilerParams(
            dimension_semantics=("parallel",),
            vmem_limit_bytes=_vmem_limit(3 * tm * k * 2, k * d * 2, 4 * tm * d * 4, 2 * tm * d * 4,
                                         n_planes * tm * LANES * 4)),
        name="dil_outproj_ln",
    )(*os, *lses, w, x, g, b)


def kernel(x, w_qkv_sb, w_o_sb, w_qkv_dil, w_o_dil, ln_mix_g, ln_mix_b, w_gate_up, w_down, ln_ffn_g, ln_ffn_b):
    batch, seq, d_model = x.shape
    depth = ln_mix_g.shape[0]
    alpha = (2.0 * depth) ** 0.25
    t = batch * seq
    sb_heads = w_o_sb.shape[1] // HEAD_DIM
    assert all(window // dilation == DIL_SUB for window, dilation in DIL_CONFIGS)

    w_qkv_first = w_qkv_sb[:1].astype(BF16)
    later_weights = ((w_qkv_sb, 1), (w_o_sb, 0), (w_qkv_dil, 0), (w_o_dil, 0), (w_gate_up, 0), (w_down, 0))
    cos2, sin2 = _rope_tables(seq)
    q_scale = LOG2_E / math.sqrt(HEAD_DIM)

    xf = x.reshape(t, d_model)
    for layer in range(depth):
        j = layer // 2
        g_mix, b_mix = ln_mix_g[layer][None, :], ln_mix_b[layer][None, :]
        if layer % 2 == 0:
            qkv = _proj_heads(xf, w_qkv_first, 0, q_scale) if layer == 0 else _proj_heads(xf, w_qkv_rest, j - 1, q_scale)
            o, rounded = _sb_attention(qkv, batch, seq, sb_heads, later_weights if layer == 0 else ())
            if layer == 0:
                w_qkv_rest, w_o_sb, w_qkv_dil, w_o_dil, w_gate_up, w_down = rounded
            xf = _outproj_ln(o, w_o_sb, j, xf, g_mix, b_mix, alpha)
        else:
            os, lses = [], []
            for group, (_, dilation) in enumerate(DIL_CONFIGS):
                q, k, v = _dil_proj(xf, w_qkv_dil, j, group, dilation, cos2, sin2, batch, q_scale)
                o, lse = _dil_attention(q, k, v)
                os.append(o)
                lses.append(lse)
            xf = _dil_outproj_ln(os, lses, w_o_dil, j, xf, g_mix, b_mix, alpha)
        h = _ffn_up(xf, w_gate_up, layer)
        xf = _ffn_down_ln(h, w_down, layer, xf, ln_ffn_g[layer][None, :], ln_ffn_b[layer][None, :], alpha)
    return xf.reshape(batch, seq, d_model)
```

```python
import functools
import math

import jax
import jax.numpy as jnp
from jax import lax
from jax.experimental import pallas as pl
from jax.experimental.pallas import tpu as pltpu

HEAD_DIM = 128
DIL_CONFIGS = ((128, 1), (512, 4), (2048, 16))
DIL_HEADS_PER_GROUP = 6
ROPE_THETA = 10000.0
LN_EPS = 1e-5
LOG2_E = math.log2(math.e)

V7X_VMEM_BYTES = 64 * 1024 * 1024
V7X_MXU_DIM = 256
LANES = 128
BF16_SUBLANES = 16
SB_MAX_LOGIT2 = 126.0
PROJ_ROWS = 1024
PROJ_COLS = 1024
LN_ROWS = 512
LN_CHUNK_ROWS = 128
FFN_COLS = 512
FFN_DOWN_ROWS = 512
SB_Q_ROWS = 512
SB_KEY_BLOCK = V7X_MXU_DIM
SB_HEADS_PER_STEP = 4
DIL_PROJ_ROWS = 512
DEINTERLEAVE_STRIDE = 4
DIL_Q_ROWS = 1024
DIL_SUB = 128

BF16 = jnp.bfloat16
F32 = jnp.float32


def _vmem_limit(*nbytes):
    return int(min(V7X_VMEM_BYTES * 7 // 8, max(sum(nbytes) * 3 // 2, V7X_VMEM_BYTES // 4)))


def _layer_norm_rows(y, g, b):
    mu = jnp.mean(y, axis=-1, keepdims=True)
    d = y - mu
    var = jnp.mean(d * d, axis=-1, keepdims=True)
    return d * lax.rsqrt(var + LN_EPS) * g + b


def _proj_heads_kernel(x_ref, w_ref, o_ref, xb_ref, *, q_tiles, q_scale):
    j = pl.program_id(1)

    @pl.when(j == 0)
    def _():
        xb_ref[...] = x_ref[...].astype(BF16)

    acc = jnp.dot(xb_ref[...], w_ref[...], preferred_element_type=F32)
    acc = acc * jnp.where(j < q_tiles, q_scale, 1.0)
    for h in range(o_ref.shape[0]):
        o_ref[h] = acc[:, h * HEAD_DIM:(h + 1) * HEAD_DIM].astype(o_ref.dtype)


def _proj_heads(x, w, layer, q_scale):
    t, k = x.shape
    n = w.shape[2]
    tm, tn = min(PROJ_ROWS, t), min(PROJ_COLS, n // 3)
    hpt = tn // HEAD_DIM
    return pl.pallas_call(
        functools.partial(_proj_heads_kernel, q_tiles=n // 3 // tn, q_scale=q_scale),
        grid=(t // tm, n // tn),
        in_specs=[pl.BlockSpec((tm, k), lambda i, j: (i, 0)),
                  pl.BlockSpec((None, k, tn), lambda i, j: (layer, 0, j))],
        out_specs=pl.BlockSpec((hpt, tm, HEAD_DIM), lambda i, j: (j, i, 0)),
        out_shape=jax.ShapeDtypeStruct((n // HEAD_DIM, t, HEAD_DIM), BF16),
        scratch_shapes=[pltpu.VMEM((tm, k), BF16)],
        compiler_params=pltpu.CompilerParams(
            dimension_semantics=("parallel", "arbitrary"),
            vmem_limit_bytes=_vmem_limit(2 * tm * k * 4, tm * k * 2, 2 * k * tn * 2, 2 * tm * tn * 2, tm * tn * 4)),
        name="sb_qkv_proj",
    )(x, w)


def _sb_attn_kernel(q_ref, k_ref, v_ref, tri_ref, *rest):
    n_cast = (len(rest) - 3) // 2
    cast_in, o_ref, cast_out = rest[:n_cast], rest[n_cast], rest[n_cast + 1:2 * n_cast + 1]
    acc_ref, run_ref = rest[2 * n_cast + 1:]
    for src, dst in zip(cast_in, cast_out):
        dst[...] = src[...].astype(dst.dtype)
    n_heads, tq, _ = q_ref.shape
    tk = SB_KEY_BLOCK
    diag_blocks = tq // tk
    i = pl.program_id(2)
    acc_ref[...] = jnp.zeros_like(acc_ref)
    run_ref[...] = jnp.zeros_like(run_ref)

    def visit(blocks, n_blocks, masked):
        units = [(h, slice(row0, row0 + rows), k0) for row0, rows, k0 in blocks for h in range(n_heads)]
        heads = range(len(units))
        zs, costs = [], []
        for h, rows, k0 in units:
            q = q_ref[h, rows, :]
            kblk = k_ref[h, pl.ds(k0, n_blocks * tk), :]
            zs.append(lax.dot_general(q, kblk, (((1,), (1,)), ((), ())), preferred_element_type=F32))
        if masked:
            causal = lax.broadcasted_iota(jnp.int32, zs[0].shape, 1) < lax.broadcasted_iota(jnp.int32, zs[0].shape, 0)
        for z in zs:
            cost = jnp.maximum(z, jnp.log(1.0 + jnp.exp2(jnp.minimum(z, SB_MAX_LOGIT2))) * LOG2_E)
            if masked:
                cost = jnp.where(causal, cost, 0.0)
            costs.append(cost.astype(BF16))
        runs = [run_ref[h, rows, :] for h, rows, _ in units]
        pieces = [[None] * n_blocks for _ in heads]
        for s in reversed(range(n_blocks)):
            cols = slice(s * tk, (s + 1) * tk)
            for h in heads:
                incl = jnp.dot(costs[h][:, cols], tri_ref[...], preferred_element_type=F32)
                pieces[h][s] = jnp.exp2(zs[h][:, cols] - incl - runs[h])
                runs[h] = runs[h] + incl[:, 0:1]
        for u, (h, rows, k0) in enumerate(units):
            w = pieces[u][0] if n_blocks == 1 else jnp.concatenate(pieces[u], axis=1)
            if masked:
                w = jnp.where(causal, w, 0.0)
            vblk = v_ref[h, pl.ds(k0, n_blocks * tk), :]
            acc_ref[h, rows, :] += jnp.dot(w.astype(BF16), vblk, preferred_element_type=F32)
            run_ref[h, rows, :] = runs[u]

    diag0 = pl.multiple_of(i * tq, tq)
    visit([(d * tk, tk, diag0 + d * tk) for d in range(diag_blocks)], 1, True)
    for d in reversed(range(diag_blocks - 1)):
        visit([((d + 1) * tk, tq - (d + 1) * tk, diag0 + d * tk)], 1, False)

    def body(n, carry):
        visit([(0, tq, pl.multiple_of((i - 1 - n) * tq, tq))], diag_blocks, False)
        return carry

    lax.fori_loop(0, i, body, 0)
    o_ref[...] = acc_ref[...].astype(o_ref.dtype)


def _cast_slab_rows(rows, n_steps):
    slab = -(-rows // n_steps)
    slab += -slab % BF16_SUBLANES
    while rows % slab:
        slab += BF16_SUBLANES
    return slab


def _sb_attention(qkv, batch, seq, n_heads, weights=()):
    tq, tk = min(SB_Q_ROWS, seq), SB_KEY_BLOCK
    nq = seq // tq
    hp = SB_HEADS_PER_STEP
    ng = n_heads // hp
    n_steps = ng * batch * nq
    tri = (lax.broadcasted_iota(jnp.int32, (tk, tk), 0) >= lax.broadcasted_iota(jnp.int32, (tk, tk), 1)).astype(BF16)
    flat = [w.reshape(-1, w.shape[-1]) for w, _ in weights]
    out_shapes = [(w.shape[0] - skip,) + w.shape[1:] for w, skip in weights]
    in_slabs, out_slabs, slab_bytes = [], [], 0
    for (w, skip), out_shape in zip(weights, out_shapes):
        rows, cols = out_shape[0] * out_shape[1], out_shape[2]
        slab = _cast_slab_rows(rows, n_steps)
        assert w.shape[1] % slab == 0
        last, first = rows // slab - 1, skip * w.shape[1] // slab
        step = lambda g, b, i, last=last: jnp.minimum((g * batch + b) * nq + i, last)
        in_slabs.append(pl.BlockSpec((slab, cols), lambda g, b, i, step=step, first=first: (first + step(g, b, i), 0)))
        out_slabs.append(pl.BlockSpec((slab, cols), lambda g, b, i, step=step: (step(g, b, i), 0)))
        slab_bytes += 2 * slab * cols * (4 + 2)
    outs = pl.pallas_call(
        _sb_attn_kernel,
        grid=(ng, batch, nq),
        in_specs=[pl.BlockSpec((hp, tq, HEAD_DIM), lambda g, b, i: (g, b * nq + i, 0)),
                  pl.BlockSpec((hp, seq, HEAD_DIM), lambda g, b, i: (ng + g, b, 0)),
                  pl.BlockSpec((hp, seq, HEAD_DIM), lambda g, b, i: (2 * ng + g, b, 0)),
                  pl.BlockSpec((tk, tk), lambda g, b, i: (0, 0))] + in_slabs,
        out_specs=[pl.BlockSpec((hp, tq, HEAD_DIM), lambda g, b, i: (g, b * nq + i, 0))] + out_slabs,
        out_shape=[jax.ShapeDtypeStruct((n_heads, batch * seq, HEAD_DIM), BF16)] +
                  [jax.ShapeDtypeStruct((s[0] * s[1], s[2]), BF16) for s in out_shapes],
        scratch_shapes=[pltpu.VMEM((hp, tq, HEAD_DIM), F32), pltpu.VMEM((hp, tq, 1), F32)],
        compiler_params=pltpu.CompilerParams(
            dimension_semantics=("arbitrary", "arbitrary", "arbitrary"),
            vmem_limit_bytes=_vmem_limit(4 * hp * seq * HEAD_DIM * 2, 8 * hp * tq * tq * 4, slab_bytes)),
        name="sb_attention",
    )(qkv, qkv, qkv, tri, *flat)
    return outs[0], [o.reshape(s) for o, s in zip(outs[1:], out_shapes)]


def _project_residual_ln(lhs, w_ref, x_ref, g_ref, b_ref, y_ref, alpha):
    rows = min(LN_CHUNK_ROWS, lhs.shape[0])
    for c in range(lhs.shape[0] // rows):
        sl = slice(c * rows, (c + 1) * rows)
        mixed = jnp.dot(lhs[sl], w_ref[...], preferred_element_type=F32)
        y_ref[sl, :] = _layer_norm_rows(alpha * x_ref[sl, :] + mixed, g_ref[...], b_ref[...])


def _outproj_ln_kernel(o_ref, w_ref, x_ref, g_ref, b_ref, y_ref, *, alpha):
    lhs = jnp.concatenate([o_ref[h] for h in range(o_ref.shape[0])], axis=1)
    _project_residual_ln(lhs, w_ref, x_ref, g_ref, b_ref, y_ref, alpha)


def _outproj_ln(o, w, layer, x, g, b, alpha):
    n_heads, t, _ = o.shape
    _, k, d = w.shape
    tm = min(LN_ROWS, t)
    return pl.pallas_call(
        functools.partial(_outproj_ln_kernel, alpha=alpha),
        grid=(t // tm,),
        in_specs=[pl.BlockSpec((n_heads, tm, HEAD_DIM), lambda i: (0, i, 0)),
                  pl.BlockSpec((None, k, d), lambda i: (layer, 0, 0), pipeline_mode=pl.Buffered(1)),
                  pl.BlockSpec((tm, d), lambda i: (i, 0)),
                  pl.BlockSpec((1, d), lambda i: (0, 0)),
                  pl.BlockSpec((1, d), lambda i: (0, 0))],
        out_specs=pl.BlockSpec((tm, d), lambda i: (i, 0)),
        out_shape=jax.ShapeDtypeStruct((t, d), F32),
        compiler_params=pltpu.CompilerParams(
            dimension_semantics=("parallel",),
            vmem_limit_bytes=_vmem_limit(2 * tm * k * 2, k * d * 2, 4 * tm * d * 4, 2 * tm * d * 4)),
        name="sb_outproj_ln",
    )(o, w, x, g, b)


def _ffn_up_kernel(x_ref, wg_ref, wu_ref, h_ref, xb_ref):
    @pl.when(pl.program_id(1) == 0)
    def _():
        xb_ref[...] = x_ref[...].astype(BF16)

    xb = xb_ref[...]
    gate = jnp.dot(xb, wg_ref[...], preferred_element_type=F32)
    up = jnp.dot(xb, wu_ref[...], preferred_element_type=F32)
    h_ref[...] = (jax.nn.silu(gate) * up).astype(h_ref.dtype)


def _ffn_up(x, w_gate_up, layer):
    t, d = x.shape
    f = w_gate_up.shape[2] // 2
    tm, tn = min(PROJ_ROWS, t), min(FFN_COLS, f)
    nj = f // tn
    return pl.pallas_call(
        _ffn_up_kernel,
        grid=(t // tm, nj),
        in_specs=[pl.BlockSpec((tm, d), lambda i, j: (i, 0)),
                  pl.BlockSpec((None, d, tn), lambda i, j: (layer, 0, j)),
                  pl.BlockSpec((None, d, tn), lambda i, j: (layer, 0, nj + j))],
        out_specs=pl.BlockSpec((tm, tn), lambda i, j: (i, j)),
        out_shape=jax.ShapeDtypeStruct((t, f), BF16),
        scratch_shapes=[pltpu.VMEM((tm, d), BF16)],
        compiler_params=pltpu.CompilerParams(
            dimension_semantics=("parallel", "arbitrary"),
            vmem_limit_bytes=_vmem_limit(2 * tm * d * 4, tm * d * 2, 4 * d * tn * 2, 2 * tm * tn * 2, 3 * tm * tn * 4)),
        name="ffn_up",
    )(x, w_gate_up, w_gate_up)


def _ffn_down_ln_kernel(h_ref, w_ref, x_ref, g_ref, b_ref, y_ref, *, alpha):
    _project_residual_ln(h_ref[...], w_ref, x_ref, g_ref, b_ref, y_ref, alpha)


def _ffn_down_ln(h, w, layer, x, g, b, alpha):
    t, f = h.shape
    d = w.shape[2]
    tm = min(FFN_DOWN_ROWS, t)
    return pl.pallas_call(
        functools.partial(_ffn_down_ln_kernel, alpha=alpha),
        grid=(t // tm,),
        in_specs=[pl.BlockSpec((tm, f), lambda i: (i, 0)),
                  pl.BlockSpec((None, f, d), lambda i: (layer, 0, 0), pipeline_mode=pl.Buffered(1)),
                  pl.BlockSpec((tm, d), lambda i: (i, 0)),
                  pl.BlockSpec((1, d), lambda i: (0, 0)),
                  pl.BlockSpec((1, d), lambda i: (0, 0))],
        out_specs=pl.BlockSpec((tm, d), lambda i: (i, 0)),
        out_shape=jax.ShapeDtypeStruct((t, d), F32),
        compiler_params=pltpu.CompilerParams(
            dimension_semantics=("parallel",),
            vmem_limit_bytes=_vmem_limit(f * d * 2, 2 * tm * f * 2, 4 * tm * d * 4, 2 * tm * d * 4)),
        name="ffn_down_ln",
    )(h, w, x, g, b)


def _rope_tables(seq):
    pos = jnp.arange(seq, dtype=F32)
    inv_freq = ROPE_THETA ** (-jnp.arange(0, HEAD_DIM, 2, dtype=F32) / HEAD_DIM)
    ang = pos[:, None] * inv_freq[None, :]
    cos, sin = jnp.cos(ang), jnp.sin(ang)
    return jnp.concatenate([cos, cos], axis=1), jnp.concatenate([-sin, sin], axis=1)


def _dil_proj_kernel(x_ref, wq_ref, wk_ref, wv_ref, cosq_ref, sinq_ref, cos_ref, sin_ref, q_ref, k_ref, v_ref,
                     xp_ref, *planes):
    _, r, tu, _ = q_ref.shape
    if r == 1:
        xp_ref[...] = x_ref[...].astype(BF16)
    else:
        src, dst = planes
        n_planes = src.shape[0]
        for j in range(n_planes):
            src[j] = x_ref[:, j * LANES:(j + 1) * LANES]
        blocks, mult, ways = [(0, src.shape[1], 0)], 1, r
        while ways > DEINTERLEAVE_STRIDE:
            split = []
            for off, n, base in blocks:
                part = n // DEINTERLEAVE_STRIDE
                for c in range(DEINTERLEAVE_STRIDE):
                    for j in range(n_planes):
                        dst[j, off + c * part:off + (c + 1) * part, :] = (
                            src[j, pl.ds(off + c, part, stride=DEINTERLEAVE_STRIDE), :])
                    split.append((off + c * part, part, base + c * mult))
            blocks, mult, ways = split, mult * DEINTERLEAVE_STRIDE, ways // DEINTERLEAVE_STRIDE
            src, dst = dst, src
        for off, n, base in blocks:
            for m in range(ways):
                c = base + m * mult
                for j in range(n_planes):
                    rows = src[j, pl.ds(off + m, tu, stride=ways), :]
                    xp_ref[c * tu:(c + 1) * tu, j * LANES:(j + 1) * LANES] = rows.astype(BF16)
    xb = xp_ref[...]
    for w_ref, out_ref, rope in ((wq_ref, q_ref, (cosq_ref, sinq_ref)), (wk_ref, k_ref, (cos_ref, sin_ref)),
                                 (wv_ref, v_ref, None)):
        acc = jnp.dot(xb, w_ref[...], preferred_element_type=F32)
        for h in range(out_ref.shape[0]):
            a = acc[:, h * HEAD_DIM:(h + 1) * HEAD_DIM]
            if rope:
                a = a * rope[0][...] + pltpu.roll(a, HEAD_DIM // 2, 1) * rope[1][...]
            a = a.astype(out_ref.dtype)
            for c in range(r):
                out_ref[h, c] = a[c * tu:(c + 1) * tu, :]


def _dil_proj(x, w, layer, group, dilation, cos2, sin2, batch, q_scale):
    t, d = x.shape
    s = t // batch
    r = dilation
    length = s // r
    tm = min(DIL_PROJ_ROWS, s)
    tu = tm // r
    nt = s // tm
    planes = [] if r == 1 else [pltpu.VMEM((d // LANES, tm, LANES), F32)] * 2
    n_groups = len(DIL_CONFIGS)
    gw = DIL_HEADS_PER_GROUP * HEAD_DIM
    cos_p = cos2.reshape(nt, tu, r, HEAD_DIM).transpose(0, 2, 1, 3).reshape(s, HEAD_DIM)
    sin_p = sin2.reshape(nt, tu, r, HEAD_DIM).transpose(0, 2, 1, 3).reshape(s, HEAD_DIM)
    w_spec = lambda kind: pl.BlockSpec((None, d, gw), lambda i: (layer, 0, kind * n_groups + group),
                                       pipeline_mode=pl.Buffered(1))
    tab_spec = pl.BlockSpec((tm, HEAD_DIM), lambda i: (i % nt, 0))
    out_spec = pl.BlockSpec((DIL_HEADS_PER_GROUP, None, r, tu, HEAD_DIM), lambda i: (0, i // nt, 0, i % nt, 0))
    out_sds = jax.ShapeDtypeStruct((DIL_HEADS_PER_GROUP, batch, r, length, HEAD_DIM), BF16)
    return pl.pallas_call(
        _dil_proj_kernel,
        grid=(t // tm,),
        in_specs=[pl.BlockSpec((tm, d), lambda i: (i, 0)),
                  w_spec(0), w_spec(1), w_spec(2), tab_spec, tab_spec, tab_spec, tab_spec],
        out_specs=[out_spec, out_spec, out_spec],
        out_shape=[out_sds, out_sds, out_sds],
        scratch_shapes=[pltpu.VMEM((tm, d), BF16)] + planes,
        compiler_params=pltpu.CompilerParams(
            dimension_semantics=("parallel",),
            vmem_limit_bytes=_vmem_limit(4 * tm * d * 4, tm * d * 2, 3 * d * gw * 2, 6 * tm * gw * 2, 2 * tm * gw * 4)),
        name=f"dil_qkv_proj_r{r}",
    )(x, w, w, w, cos_p * q_scale, sin_p * q_scale, cos_p, sin_p)


def _dil_attn_kernel(q_ref, k_ref, v_ref, kp_ref, vp_ref, o_ref, lse_ref):
    n_heads, n_res, tu, _ = q_ref.shape
    sub = DIL_SUB
    iu = pl.program_id(2)
    qi = lax.broadcasted_iota(jnp.int32, (sub, 2 * sub), 0)
    kj = lax.broadcasted_iota(jnp.int32, (sub, 2 * sub), 1)
    in_window = (kj >= qi) & (kj <= qi + sub)
    first_valid = in_window & (kj >= jnp.where(iu > 0, 0, sub))
    lane = lax.broadcasted_iota(jnp.int32, (sub, LANES), 1)
    for c in range(n_res):
        for s in range(tu // sub):
            rows = slice(s * sub, (s + 1) * sub)
            valid = first_valid if s == 0 else in_window
            lse_tile = jnp.zeros((sub, LANES), F32)
            for h in range(n_heads):
                q = q_ref[h, c, rows, :]
                if s == 0:
                    k_prev, v_prev = kp_ref[h, c], vp_ref[h, c]
                else:
                    k_prev, v_prev = k_ref[h, c, (s - 1) * sub:s * sub, :], v_ref[h, c, (s - 1) * sub:s * sub, :]
                keys = jnp.concatenate([k_prev, k_ref[h, c, rows, :]], axis=0)
                vals = jnp.concatenate([v_prev, v_ref[h, c, rows, :]], axis=0)
                sc = lax.dot_general(q, keys, (((1,), (1,)), ((), ())), preferred_element_type=F32)
                sc = jnp.where(valid, sc, -jnp.inf)
                m = jnp.max(sc, axis=1, keepdims=True)
                p = jnp.exp2(sc - m)
                den = jnp.sum(p, axis=1, keepdims=True)
                o = jnp.dot(p.astype(BF16), vals, preferred_element_type=F32) / den
                o_ref[c, rows, h * HEAD_DIM:(h + 1) * HEAD_DIM] = o.astype(o_ref.dtype)
                lse_tile = jnp.where(lane == h, m + jnp.log(den) * LOG2_E, lse_tile)
            lse_ref[c, rows, :] = lse_tile


def _dil_attention(q, k, v):
    n_heads, b, r, length, _ = q.shape
    tu = min(DIL_Q_ROWS, length)
    rc = min(r, DIL_Q_ROWS // tu)
    sub = DIL_SUB
    gw = n_heads * HEAD_DIM
    cur = pl.BlockSpec((n_heads, None, rc, tu, HEAD_DIM), lambda bi, c, iu: (0, bi, c, iu, 0))
    prev = pl.BlockSpec((n_heads, None, rc, sub, HEAD_DIM),
                        lambda bi, c, iu: (0, bi, c, jnp.maximum(iu * (tu // sub) - 1, 0), 0))
    return pl.pallas_call(
        _dil_attn_kernel,
        grid=(b, r // rc, length // tu),
        in_specs=[cur, cur, cur, prev, prev],
        out_specs=[pl.BlockSpec((None, rc, tu, gw), lambda bi, c, iu: (bi, c, iu, 0)),
                   pl.BlockSpec((None, rc, tu, LANES), lambda bi, c, iu: (bi, c, iu, 0))],
        out_shape=[jax.ShapeDtypeStruct((b, r, length, gw), BF16),
                   jax.ShapeDtypeStruct((b, r, length, LANES), F32)],
        compiler_params=pltpu.CompilerParams(
            dimension_semantics=("parallel", "parallel", "parallel"),
            vmem_limit_bytes=_vmem_limit(6 * rc * tu * gw * 2, 4 * rc * sub * gw * 2, 2 * rc * tu * gw * 2,
                                         2 * rc * tu * LANES * 4)),
        name=f"dil_attention_r{r}",
    )(q, k, v, k, v)


def _dil_outproj_ln_kernel(o0_ref, o1_ref, o2_ref, l0_ref, l1_ref, l2_ref, w_ref, x_ref, g_ref, b_ref, y_ref,
                           planes_ref, *, alpha):
    o_refs, l_refs = (o0_ref, o1_ref, o2_ref), (l0_ref, l1_ref, l2_ref)
    n_o = len(o_refs) * DIL_HEADS_PER_GROUP

    def natural(ref, cols, plane):
        r, tu, _ = ref.shape
        if r == 1:
            return ref[0, :, cols].astype(F32)
        if r <= DEINTERLEAVE_STRIDE:
            for c in range(r):
                planes_ref[plane, pl.ds(c, tu, stride=r), :] = ref[c, :, cols].astype(F32)
            return planes_ref[plane]
        narrow, wide = DEINTERLEAVE_STRIDE, r // DEINTERLEAVE_STRIDE
        stage, part = planes_ref.shape[0] - 1, tu * wide
        for c in range(r):
            lo, hi = c % narrow, c // narrow
            planes_ref[stage, pl.ds(lo * part + hi, tu, stride=wide), :] = ref[c, :, cols].astype(F32)
        for lo in range(narrow):
            planes_ref[plane, pl.ds(lo, part, stride=narrow), :] = planes_ref[stage, lo * part:(lo + 1) * part, :]
        return planes_ref[plane]

    lses = [natural(l_ref, slice(0, LANES), n_o + g) for g, l_ref in enumerate(l_refs)]
    mx = jnp.maximum(jnp.maximum(lses[0], lses[1]), lses[2])
    es = [jnp.exp2(l - mx) for l in lses]
    den = es[0] + es[1] + es[2]
    pieces = []
    for g, (o_ref, e) in enumerate(zip(o_refs, es)):
        share = e / den
        for h in range(DIL_HEADS_PER_GROUP):
            o = natural(o_ref, slice(h * HEAD_DIM, (h + 1) * HEAD_DIM), g * DIL_HEADS_PER_GROUP + h)
            pieces.append((o * share[:, h:h + 1]).astype(BF16))
    _project_residual_ln(jnp.concatenate(pieces, axis=1), w_ref, x_ref, g_ref, b_ref, y_ref, alpha)


def _dil_outproj_ln(os, lses, w, layer, x, g, b, alpha):
    t, d = x.shape
    _, k, _ = w.shape
    batch = os[0].shape[0]
    tm = min(LN_ROWS, t // batch)
    nt = t // batch // tm

    def res_spec(a):
        _, r, _, width = a.shape
        return pl.BlockSpec((None, r, tm // r, width), lambda i: (i // nt, 0, i % nt, 0))

    n_planes = len(os) * DIL_HEADS_PER_GROUP + len(lses) + 1
    return pl.pallas_call(
        functools.partial(_dil_outproj_ln_kernel, alpha=alpha),
        grid=(t // tm,),
        in_specs=[res_spec(a) for a in (*os, *lses)] + [
                  pl.BlockSpec((None, k, d), lambda i: (layer, 0, 0), pipeline_mode=pl.Buffered(1)),
                  pl.BlockSpec((tm, d), lambda i: (i, 0)),
                  pl.BlockSpec((1, d), lambda i: (0, 0)),
                  pl.BlockSpec((1, d), lambda i: (0, 0))],
        out_specs=pl.BlockSpec((tm, d), lambda i: (i, 0)),
        out_shape=jax.ShapeDtypeStruct((t, d), F32),
        scratch_shapes=[pltpu.VMEM((n_planes, tm, LANES), F32)],
        compiler_params=pltpu.CompilerParams(
            dimension_semantics=("parallel",),
            vmem_limit_bytes=_vmem_limit(3 * tm * k * 2, k * d * 2, 4 * tm * d * 4, 2 * tm * d * 4,
                                         n_planes * tm * LANES * 4)),
        name="dil_outproj_ln",
    )(*os, *lses, w, x, g, b)


def kernel(x, w_qkv_sb, w_o_sb, w_qkv_dil, w_o_dil, ln_mix_g, ln_mix_b, w_gate_up, w_down, ln_ffn_g, ln_ffn_b):
    batch, seq, d_model = x.shape
    depth = ln_mix_g.shape[0]
    alpha = (2.0 * depth) ** 0.25
    t = batch * seq
    sb_heads = w_o_sb.shape[1] // HEAD_DIM
    assert all(window // dilation == DIL_SUB for window, dilation in DIL_CONFIGS)

    w_qkv_first = w_qkv_sb[:1].astype(BF16)
    later_weights = ((w_qkv_sb, 1), (w_o_sb, 0), (w_qkv_dil, 0), (w_o_dil, 0), (w_gate_up, 0), (w_down, 0))
    cos2, sin2 = _rope_tables(seq)
    q_scale = LOG2_E / math.sqrt(HEAD_DIM)

    xf = x.reshape(t, d_model)
    for layer in range(depth):
        j = layer // 2
        g_mix, b_mix = ln_mix_g[layer][None, :], ln_mix_b[layer][None, :]
        if layer % 2 == 0:
            qkv = _proj_heads(xf, w_qkv_first, 0, q_scale) if layer == 0 else _proj_heads(xf, w_qkv_rest, j - 1, q_scale)
            o, rounded = _sb_attention(qkv, batch, seq, sb_heads, later_weights if layer == 0 else ())
            if layer == 0:
                w_qkv_rest, w_o_sb, w_qkv_dil, w_o_dil, w_gate_up, w_down = rounded
            xf = _outproj_ln(o, w_o_sb, j, xf, g_mix, b_mix, alpha)
        else:
            os, lses = [], []
            for group, (_, dilation) in enumerate(DIL_CONFIGS):
                q, k, v = _dil_proj(xf, w_qkv_dil, j, group, dilation, cos2, sin2, batch, q_scale)
                o, lse = _dil_attention(q, k, v)
                os.append(o)
                lses.append(lse)
            xf = _dil_outproj_ln(os, lses, w_o_dil, j, xf, g_mix, b_mix, alpha)
        h = _ffn_up(xf, w_gate_up, layer)
        xf = _ffn_down_ln(h, w_down, layer, xf, ln_ffn_g[layer][None, :], ln_ffn_b[layer][None, :], alpha)
    return xf.reshape(batch, seq, d_model)
```
